```python
import jax
import jax.numpy as jnp
from jax import lax
import numpy as np

D_MODEL = 1024
BATCH = 8
SEQ = 4096
DEPTH = 4

CTX_LEN = 256
GRID_W = 64
D_FF = 2816
GLA_HEADS = 4
GLA_DK = 128
GLA_DV = 256
GLA_KEY = GLA_HEADS * GLA_DK
GLA_VAL = GLA_HEADS * GLA_DV
GLA_GATE_RANK = 16
GLA_GATE_TEMP = 16.0
GLA_CHUNK = 64
CONV_WIDTH = 1024
MLA_HEADS = 8
MLA_Q_RANK = 512
MLA_KV_RANK = 256
MLA_NOPE = 128
MLA_ROPE = 64
MLA_V = 128
ATTN_BLOCK = 128
ROPE_THETA = 10000.0
N_BRANCH = 3
N_MOD = 9
LN_EPS = 1e-6
DEEPNORM_ALPHA = (2.0 * DEPTH) ** 0.25
DEEPNORM_BETA = (8.0 * DEPTH) ** -0.25
IN_SIZES = (GLA_KEY, GLA_KEY, GLA_VAL, GLA_VAL, GLA_GATE_RANK, GLA_GATE_RANK,
            CONV_WIDTH, CONV_WIDTH, CONV_WIDTH,
            MLA_Q_RANK, MLA_KV_RANK, MLA_ROPE,
            N_BRANCH * D_MODEL)
IN_WIDTH = GLA_KEY * 2 + GLA_VAL * 2 + GLA_GATE_RANK * 2 + CONV_WIDTH * 3 + MLA_Q_RANK + MLA_KV_RANK + MLA_ROPE + N_BRANCH * D_MODEL

kernel_name = "hybrid_gla_shortconv_mla_dit_trunk"


def _layer_norm(x, w, b):
    xf = x.astype(jnp.float32)
    mu = jnp.mean(xf, axis=-1, keepdims=True)
    var = jnp.mean(jnp.square(xf - mu), axis=-1, keepdims=True)
    return ((xf - mu) * lax.rsqrt(var + LN_EPS) * w + b).astype(x.dtype)


def _rms_norm(x, w):
    xf = x.astype(jnp.float32)
    return (xf * lax.rsqrt(jnp.mean(jnp.square(xf), axis=-1, keepdims=True) + LN_EPS) * w).astype(x.dtype)


def _modulate(x, shift, scale):
    return x * (1.0 + scale) + shift


def _swiglu(h, w13, w2):
    a, g = jnp.split(h @ w13, 2, axis=-1)
    return (jax.nn.silu(a) * g) @ w2


def _split(z, sizes):
    idx, acc = [], 0
    for s in sizes[:-1]:
        acc += s
        idx.append(acc)
    return jnp.split(z, idx, axis=-1)


def _heads(t, n):
    b, l, _ = t.shape
    return t.reshape(b, l, n, -1).transpose(0, 2, 1, 3)


def _merge(t):
    b, h, l, d = t.shape
    return t.transpose(0, 2, 1, 3).reshape(b, l, h * d)


def _grid_positions(n):
    rows_n = n // GRID_W
    rows = jnp.repeat(jnp.arange(rows_n, dtype=jnp.int32), GRID_W)
    cols = jnp.tile(jnp.arange(GRID_W, dtype=jnp.int32), rows_n)
    return rows, cols


def _rope_1d(x, pos):
    half = x.shape[-1] // 2
    inv_freq = ROPE_THETA ** (-jnp.arange(half, dtype=jnp.float32) / half)
    ang = pos.astype(jnp.float32)[:, None] * inv_freq
    cos = jnp.cos(ang).astype(x.dtype)
    sin = jnp.sin(ang).astype(x.dtype)
    x1, x2 = x[..., :half], x[..., half:]
    return jnp.concatenate([x1 * cos - x2 * sin, x1 * sin + x2 * cos], axis=-1)


def _rope_2d(x, rows, cols):
    h = x.shape[-1] // 2
    return jnp.concatenate([_rope_1d(x[..., :h], rows), _rope_1d(x[..., h:], cols)], axis=-1)


def _log_decay(glr, w, b):
    return jax.nn.log_sigmoid((glr @ w + b).astype(jnp.float32)) / GLA_GATE_TEMP


def _gla_scan(q, k, v, g, s0):
    b_, h, l, _ = q.shape
    dv = v.shape[-1]
    n = l // GLA_CHUNK

    def to_chunks(t):
        return jnp.moveaxis(t.reshape(b_, h, n, GLA_CHUNK, t.shape[-1]), 2, 0).astype(jnp.float32)

    mask = jnp.tril(jnp.ones((GLA_CHUNK, GLA_CHUNK), dtype=bool))

    def step(s, inp):
        qc, kc, vc, gc = inp
        bcum = jnp.cumsum(gc, axis=-2)
        b_last = bcum[..., -1:, :]
        qe = qc * jnp.exp(bcum)
        ke = kc * jnp.exp(-bcum)
        a = jnp.where(mask, jnp.einsum('bhid,bhjd->bhij', qe, ke), 0.0)
        o = jnp.einsum('bhij,bhjv->bhiv', a, vc) + jnp.einsum('bhid,bhdv->bhiv', qe, s)
        s = jnp.exp(b_last[..., 0, :])[..., None] * s + jnp.einsum('bhjd,bhjv->bhdv', kc * jnp.exp(b_last - bcum), vc)
        return s, o

    s, o = lax.scan(step, s0, (to_chunks(q), to_chunks(k), to_chunks(v), to_chunks(g)))
    o = jnp.moveaxis(o, 0, 2).reshape(b_, h, l, dv)
    return o.astype(v.dtype), s


def _gla_bidir(q, k, v, g_fwd, g_bwd, s_fwd0, s_bwd0):
    flip = lambda t: jnp.flip(t, axis=2)
    o_f, s_f = _gla_scan(q, k, v, g_fwd, s_fwd0)
    o_b, s_b = _gla_scan(flip(q), flip(k), flip(v), flip(g_bwd), s_bwd0)
    return o_f + flip(o_b), s_f, s_b


def _attend(q, k, v):
    s = jnp.einsum('bhqd,bhkd->bhqk', q, k).astype(jnp.float32) * (q.shape[-1] ** -0.5)
    p = jax.nn.softmax(s, axis=-1).astype(v.dtype)
    return jnp.einsum('bhqk,bhkd->bhqd', p, v)


def _attend_blocks(q, k, v):
    b, h, l, d = q.shape
    nb = l // ATTN_BLOCK
    qb = jnp.moveaxis(q.reshape(b, h, nb, ATTN_BLOCK, d), 2, 0)
    ob = lax.map(lambda qi: _attend(qi, k, v), qb)
    return jnp.moveaxis(ob, 0, 2).reshape(b, h, l, v.shape[-1])


def _short_conv(u, w):
    up = jnp.pad(u, ((0, 0), (1, 1), (0, 0)))
    return up[:, :-2] * w[0] + up[:, 1:-1] * w[1] + up[:, 2:] * w[2]


def _mixer(hx, hc, rows, cols, w_in, b_gate, gla_decay_w, gla_decay_b, gla_norm, gla_proj,
           conv_w, conv_proj, mla_q_norm, mla_w_uq, mla_kv_norm, mla_w_ukv, mla_proj, w_out,
           with_ctx_out):
    zx = _split(hx @ w_in, IN_SIZES)
    zc = _split(hc @ w_in, IN_SIZES)

    def gla_in(z):
        lf = _log_decay(z[4], gla_decay_w[0], gla_decay_b[0])
        lb = _log_decay(z[5], gla_decay_w[1], gla_decay_b[1])
        return (_heads(z[0], GLA_HEADS) * (GLA_DK ** -0.5), _heads(z[1], GLA_HEADS), _heads(z[2], GLA_HEADS),
                _heads(lf, GLA_HEADS), _heads(lb, GLA_HEADS))

    def gla_out(o, r):
        o = _rms_norm(o, gla_norm.reshape(GLA_HEADS, 1, GLA_DV))
        return (_merge(o) * jax.nn.silu(r)) @ gla_proj

    qc, kc, vc, lfc, lbc = gla_in(zc)
    s_init = jnp.zeros(qc.shape[:2] + (GLA_DK, GLA_DV), jnp.float32)
    o_gla_c, s_fwd_ctx, s_bwd_ctx = _gla_bidir(qc, kc, vc, lfc, lbc, s_init, s_init)
    qx, kx, vx, lfx, lbx = gla_in(zx)
    o_gla_x, _, _ = _gla_bidir(qx, kx, vx, lfx, lbx, s_fwd_ctx, s_bwd_ctx)
    a_x = gla_out(o_gla_x, zx[3])

    b_x = (zx[6] * _short_conv(zx[7] * zx[8], conv_w)) @ conv_proj

    def mla_q(cq):
        return _heads(_rms_norm(cq, mla_q_norm) @ mla_w_uq, MLA_HEADS)

    def mla_kv(ckv, k_rope):
        kv = _heads(_rms_norm(ckv, mla_kv_norm) @ mla_w_ukv, MLA_HEADS)
        k_nope, v = kv[..., :MLA_NOPE], kv[..., MLA_NOPE:]
        k_rope = jnp.broadcast_to(k_rope[:, None], k_nope.shape[:3] + (MLA_ROPE,))
        return jnp.concatenate([k_nope, k_rope], axis=-1), v

    q_lat = mla_q(zx[9])
    q_lat = jnp.concatenate([q_lat[..., :MLA_NOPE], _rope_2d(q_lat[..., MLA_NOPE:], rows, cols)], axis=-1)
    k_lat, v_lat = mla_kv(zx[10], _rope_2d(zx[11], rows, cols))
    k_ctx, v_ctx = mla_kv(zc[10], zc[11])
    k_all = jnp.concatenate([k_ctx, k_lat], axis=2)
    v_all = jnp.concatenate([v_ctx, v_lat], axis=2)
    c_x = _merge(_attend_blocks(q_lat, k_all, v_all)) @ mla_proj

    def merge(a, b, c_, zg):
        ga, gb, gc = jnp.split(jax.nn.sigmoid(zg + b_gate), N_BRANCH, axis=-1)
        return (ga * a + gb * b + gc * c_) @ w_out

    out_x = merge(a_x, b_x, c_x, zx[12])
    if not with_ctx_out:
        return out_x, None
    a_c = gla_out(o_gla_c, zc[3])
    b_c = (zc[6] * _short_conv(zc[7] * zc[8], conv_w)) @ conv_proj
    c_c = _merge(_attend(mla_q(zc[9]), k_ctx, v_ctx)) @ mla_proj
    out_c = merge(a_c, b_c, c_c, zc[12])
    return out_x, out_c


def _ffn_step(h, m, i, w13, w2, lw, lb):
    shift, scale, gate = m[3 * i], m[3 * i + 1], m[3 * i + 2]
    return _layer_norm(DEEPNORM_ALPHA * h + 0.5 * gate * _swiglu(_modulate(h, shift, scale), w13, w2), lw, lb)


def setup_inputs(seed: int = 0) -> dict:
    key = jax.random.key(seed)
    ks = jax.random.split(key, 32)
    nrm = lambda k, shape, s: jax.random.normal(k, shape, jnp.float32) * s
    d = D_MODEL
    return {
        "x": nrm(ks[0], (BATCH, SEQ, d), 1.0),
        "c": nrm(ks[1], (BATCH, d), 1.0),
        "ctx": nrm(ks[2], (BATCH, CTX_LEN, d), 1.0),
        "c_ctx": nrm(ks[3], (d,), 1.0),
        "ada_w": nrm(ks[4], (DEPTH, d, N_MOD * d), d ** -0.5),
        "ada_b": nrm(ks[5], (DEPTH, N_MOD * d), 0.01),
        "ln_w": 1.0 + nrm(ks[6], (DEPTH, 3, d), 0.01),
        "ln_b": nrm(ks[7], (DEPTH, 3, d), 0.01),
        "ffn_w13": nrm(ks[8], (DEPTH, 2, d, 2 * D_FF), d ** -0.5),
        "ffn_w2": nrm(ks[9], (DEPTH, 2, D_FF, d), D_FF ** -0.5 * DEEPNORM_BETA),
        "w_in": nrm(ks[10], (DEPTH, d, IN_WIDTH), d ** -0.5),
        "b_gate": nrm(ks[11], (DEPTH, N_BRANCH * d), 0.01),
        "gla_decay_w": nrm(ks[12], (DEPTH, 2, GLA_GATE_RANK, GLA_KEY), GLA_GATE_RANK ** -0.5),
        "gla_decay_b": nrm(ks[13], (DEPTH, 2, GLA_KEY), 0.01),
        "gla_norm": 1.0 + nrm(ks[14], (DEPTH, GLA_VAL), 0.01),
        "gla_proj": nrm(ks[15], (DEPTH, GLA_VAL, d), GLA_VAL ** -0.5),
        "conv_w": nrm(ks[16], (DEPTH, 3, CONV_WIDTH), 3 ** -0.5),
        "conv_proj": nrm(ks[17], (DEPTH, CONV_WIDTH, d), CONV_WIDTH ** -0.5),
        "mla_q_norm": 1.0 + nrm(ks[18], (DEPTH, MLA_Q_RANK), 0.01),
        "mla_w_uq": nrm(ks[19], (DEPTH, MLA_Q_RANK, MLA_HEADS * (MLA_NOPE + MLA_ROPE)), MLA_Q_RANK ** -0.5),
        "mla_kv_norm": 1.0 + nrm(ks[20], (DEPTH, MLA_KV_RANK), 0.01),
        "mla_w_ukv": nrm(ks[21], (DEPTH, MLA_KV_RANK, MLA_HEADS * (MLA_NOPE + MLA_V)), MLA_KV_RANK ** -0.5),
        "mla_proj": nrm(ks[22], (DEPTH, MLA_HEADS * MLA_V, d), (MLA_HEADS * MLA_V) ** -0.5),
        "w_out": nrm(ks[23], (DEPTH, d, d), d ** -0.5 * DEEPNORM_BETA),
    }


def reference(x, c, ctx, c_ctx, ada_w, ada_b, ln_w, ln_b, ffn_w13, ffn_w2, w_in, b_gate,
              gla_decay_w, gla_decay_b, gla_norm, gla_proj, conv_w, conv_proj,
              mla_q_norm, mla_w_uq, mla_kv_norm, mla_w_ukv, mla_proj, w_out):
    rows, cols = _grid_positions(x.shape[1])
    sc = jax.nn.silu(c)
    scc = jax.nn.silu(c_ctx)
    for l in range(DEPTH):
        last = l == DEPTH - 1
        mod_x = jnp.split((sc @ ada_w[l] + ada_b[l])[:, None, :], N_MOD, axis=-1)
        mod_c = jnp.split((scc @ ada_w[l] + ada_b[l])[None, None, :], N_MOD, axis=-1)
        x = _ffn_step(x, mod_x, 0, ffn_w13[l, 0], ffn_w2[l, 0], ln_w[l, 0], ln_b[l, 0])
        ctx = _ffn_step(ctx, mod_c, 0, ffn_w13[l, 0], ffn_w2[l, 0], ln_w[l, 0], ln_b[l, 0])
        mx, mc = _mixer(_modulate(x, mod_x[3], mod_x[4]), _modulate(ctx, mod_c[3], mod_c[4]), rows, cols,
                        w_in[l], b_gate[l], gla_decay_w[l], gla_decay_b[l], gla_norm[l], gla_proj[l],
                        conv_w[l], conv_proj[l], mla_q_norm[l], mla_w_uq[l], mla_kv_norm[l], mla_w_ukv[l],
                        mla_proj[l], w_out[l], not last)
        x = _layer_norm(DEEPNORM_ALPHA * x + mod_x[5] * mx, ln_w[l, 1], ln_b[l, 1])
        x = _ffn_step(x, mod_x, 2, ffn_w13[l, 1], ffn_w2[l, 1], ln_w[l, 2], ln_b[l, 2])
        if not last:
            ctx = _layer_norm(DEEPNORM_ALPHA * ctx + mod_c[5] * mc, ln_w[l, 1], ln_b[l, 1])
            ctx = _ffn_step(ctx, mod_c, 2, ffn_w13[l, 1], ffn_w2[l, 1], ln_w[l, 2], ln_b[l, 2])
    return x
```

```python
import functools

import jax
import jax.numpy as jnp
from jax import lax
from jax.experimental import pallas as pl
from jax.experimental.pallas import tpu as pltpu

F32 = jnp.float32
BF16 = jnp.bfloat16

D = 1024
B = 8
L = 4096
DEPTH = 4
LC = 256
GRID_W = 64
DFF = 2816
GH, GDK, GDV = 4, 128, 256
GKEY, GVAL = GH * GDK, GH * GDV
GRANK = 16
GTEMP = 16.0
GCHUNK = 64
CW = 1024
MH, MQR, MKVR, MNOPE, MROPE, MV = 8, 512, 256, 128, 64, 128
THETA = 10000.0
NMOD = 9
EPS = 1e-6
ALPHA = (2.0 * DEPTH) ** 0.25

NX = B * L
NC = B * LC
NT = NX + NC
MODROWS = 16
MHD = 256

VMEM_LIMIT = 56 * 1024 * 1024


def _cparams(sem):
    return pltpu.CompilerParams(dimension_semantics=sem, vmem_limit_bytes=VMEM_LIMIT)


def _const(shape):
    n = len(shape)
    return pl.BlockSpec(shape, lambda *_: (0,) * n, pipeline_mode=pl.Buffered(1))


def _mod_spec(tm):
    per = L // tm
    return pl.BlockSpec((None, 1, D), lambda i: (jnp.minimum(i // per, B), 0, 0))


def _silu(x):
    return x * jax.nn.sigmoid(x)


def _layer_norm(z, w, b):
    mu = jnp.mean(z, axis=-1, keepdims=True)
    zc = z - mu
    var = jnp.mean(zc * zc, axis=-1, keepdims=True)
    return zc * lax.rsqrt(var + EPS) * w + b


def _rms(x, w):
    return x * lax.rsqrt(jnp.mean(x * x, axis=-1, keepdims=True) + EPS) * w


def _dot(a, b):
    return jnp.dot(a, b, preferred_element_type=F32)


def _dot_nt(a, b):
    return lax.dot_general(a, b, (((1,), (1,)), ((), ())), preferred_element_type=F32)


def _dot_tn(a, b):
    return lax.dot_general(a, b, (((0,), (0,)), ((), ())), preferred_element_type=F32)


def _ada_kernel(c_ref, w_ref, b_ref, o_ref):
    sc = _silu(c_ref[...]).astype(BF16)
    o_ref[...] = _dot(sc, w_ref[...].astype(BF16)) + b_ref[...]


def _ada_call(c_all, ada_w, ada_b):
    tn = 1152
    return pl.pallas_call(
        _ada_kernel,
        grid=(DEPTH, NMOD * D // tn),
        in_specs=[
            pl.BlockSpec((MODROWS, D), lambda l, j: (0, 0)),
            pl.BlockSpec((None, D, tn), lambda l, j: (l, 0, j)),
            pl.BlockSpec((None, 1, tn), lambda l, j: (l, 0, j)),
        ],
        out_specs=pl.BlockSpec((None, MODROWS, tn), lambda l, j: (l, 0, j)),
        out_shape=jax.ShapeDtypeStruct((DEPTH, MODROWS, NMOD * D), F32),
        compiler_params=_cparams(("arbitrary", "arbitrary")),
        name="ada",
    )(c_all, ada_w, ada_b.reshape(DEPTH, 1, NMOD * D))


def _ffn_kernel(x_ref, sh_ref, sc_ref, g_ref, w13_ref, w2_ref, lw_ref, lb_ref, *rest, emit_mod):
    x = x_ref[...]
    hm = (x * (1.0 + sc_ref[...]) + sh_ref[...]).astype(BF16)
    a = _dot(hm, w13_ref[:, :DFF])
    g = _dot(hm, w13_ref[:, DFF:])
    act = (_silu(a) * g).astype(BF16)
    y = _dot(act, w2_ref[...])
    z = ALPHA * x + (0.5 * g_ref[...]) * y
    out = _layer_norm(z, lw_ref[...], lb_ref[...])
    if emit_mod:
        sh2_ref, sc2_ref, o_ref, om_ref = rest
        o_ref[...] = out
        om_ref[...] = (out * (1.0 + sc2_ref[...]) + sh2_ref[...]).astype(BF16)
    else:
        (o_ref,) = rest
        o_ref[...] = out


def _ffn_call(h, rows, mod3, w13, w2, lw, lb, mod_next=None):
    tm = 512
    emit = mod_next is not None
    row = pl.BlockSpec((tm, D), lambda i: (i, 0))
    in_specs = [row, _mod_spec(tm), _mod_spec(tm), _mod_spec(tm),
                _const((D, 2 * DFF)), _const((DFF, D)), _const((1, D)), _const((1, D))]
    args = [h, *mod3, w13, w2, lw, lb]
    out_shape = [jax.ShapeDtypeStruct((rows, D), F32)]
    out_specs = [row]
    if emit:
        in_specs += [_mod_spec(tm), _mod_spec(tm)]
        args += list(mod_next)
        out_shape.append(jax.ShapeDtypeStruct((rows, D), BF16))
        out_specs.append(row)
    res = pl.pallas_call(
        functools.partial(_ffn_kernel, emit_mod=emit),
        grid=(rows // tm,),
        in_specs=in_specs,
        out_specs=out_specs,
        out_shape=out_shape,
        compiler_params=_cparams(("arbitrary",)),
        name="ffn",
    )(*args)
    return res if emit else res[0]


def _p1_kernel(hm_ref, w_ref, dw_ref, db_ref, qk_ref, v_ref, r_ref, lfb_ref):
    z = _dot(hm_ref[...], w_ref[...])
    qk_ref[:, :GKEY] = (z[:, :GKEY] * (GDK ** -0.5)).astype(BF16)
    qk_ref[:, GKEY:] = z[:, GKEY:2 * GKEY].astype(BF16)
    v_ref[...] = z[:, 2 * GKEY:2 * GKEY + GVAL].astype(BF16)
    r_ref[...] = _silu(z[:, 2 * GKEY + GVAL:2 * GKEY + 2 * GVAL]).astype(BF16)
    glr = z[:, 2 * GKEY + 2 * GVAL:].astype(BF16)
    xg = _dot(glr, dw_ref[...]) + db_ref[...]
    lfb_ref[...] = (jnp.minimum(xg, 0.0) - jnp.log1p(jnp.exp(-jnp.abs(xg)))) * (1.0 / GTEMP)


def _p1_call(hm, w, dw, db):
    tm = 512
    nw = w.shape[1]
    return pl.pallas_call(
        _p1_kernel,
        grid=(NT // tm,),
        in_specs=[pl.BlockSpec((tm, D), lambda i: (i, 0)),
                  _const((D, nw)), _const((128, 2 * GKEY)), _const((1, 2 * GKEY))],
        out_specs=[pl.BlockSpec((tm, 2 * GKEY), lambda i: (i, 0)),
                   pl.BlockSpec((tm, GVAL), lambda i: (i, 0)),
                   pl.BlockSpec((tm, GVAL), lambda i: (i, 0)),
                   pl.BlockSpec((tm, 2 * GKEY), lambda i: (i, 0))],
        out_shape=[jax.ShapeDtypeStruct((NT, 2 * GKEY), BF16),
                   jax.ShapeDtypeStruct((NT, GVAL), BF16),
                   jax.ShapeDtypeStruct((NT, GVAL), BF16),
                   jax.ShapeDtypeStruct((NT, 2 * GKEY), F32)],
        compiler_params=_cparams(("arbitrary",)),
        name="proj_gla",
    )(hm, w, dw, db)


def _p2_kernel(hm_ref, w_ref, zb_ref, u_ref):
    z = _dot(hm_ref[...], w_ref[...])
    zb_ref[...] = z[:, :CW].astype(BF16)
    u_ref[...] = (z[:, CW:2 * CW] * z[:, 2 * CW:]).astype(BF16)


def _p2_call(hm, w, rows):
    tm = 512
    return pl.pallas_call(
        _p2_kernel,
        grid=(rows // tm,),
        in_specs=[pl.BlockSpec((tm, D), lambda i: (i, 0)), _const((D, 3 * CW))],
        out_specs=[pl.BlockSpec((tm, CW), lambda i: (i, 0)),
                   pl.BlockSpec((tm, CW), lambda i: (i, 0))],
        out_shape=[jax.ShapeDtypeStruct((rows, CW), BF16),
                   jax.ShapeDtypeStruct((rows, CW), BF16)],
        compiler_params=_cparams(("arbitrary",)),
        name="proj_conv",
    )(hm, w)


def _p3_kernel(hm_ref, w_ref, b_ref, g_ref):
    z = _dot(hm_ref[...], w_ref[...]) + b_ref[...]
    g_ref[...] = jax.nn.sigmoid(z).astype(BF16)


def _p3_call(hm, w, b_gate, rows):
    tm = 512
    return pl.pallas_call(
        _p3_kernel,
        grid=(rows // tm,),
        in_specs=[pl.BlockSpec((tm, D), lambda i: (i, 0)), _const((D, 3 * D)), _const((1, 3 * D))],
        out_specs=pl.BlockSpec((tm, 3 * D), lambda i: (i, 0)),
        out_shape=jax.ShapeDtypeStruct((rows, 3 * D), BF16),
        compiler_params=_cparams(("arbitrary",)),
        name="proj_gate",
    )(hm, w, b_gate)


def _rope_fold(t, cs):
    rr = t * cs
    return rr + pltpu.roll(rr, MROPE, 1)


def _p4_kernel(hm_ref, w_ref, cs_ref, qn_ref, wuq_ref, kvn_ref, wukv_ref, q_ref, k_ref, v_ref):
    z = _dot(hm_ref[...], w_ref[...])
    cs = cs_ref[...]
    cqn = _rms(z[:, :MQR], qn_ref[...]).astype(BF16)
    qz = _dot(cqn, wuq_ref[...]) * ((MNOPE + MROPE) ** -0.5)
    for h in range(MH):
        base = h * MHD
        rot = _rope_fold(qz[:, base + MNOPE:base + MHD], cs)
        q_ref[h] = jnp.concatenate([qz[:, base:base + MNOPE], rot], axis=1).astype(BF16)
    kvn = _rms(z[:, MQR:MQR + MKVR], kvn_ref[...]).astype(BF16)
    kvz = _dot(kvn, wukv_ref[...])
    krot = _rope_fold(z[:, MQR + MKVR:], cs)
    lane = lax.broadcasted_iota(jnp.int32, krot.shape, 1)
    krot = jnp.where(lane < MROPE, krot, 0.0)
    for h in range(MH):
        k_ref[h] = jnp.concatenate([kvz[:, h * MNOPE:(h + 1) * MNOPE], krot], axis=1).astype(BF16)
        v_ref[h] = kvz[:, MH * MNOPE + h * MV:MH * MNOPE + (h + 1) * MV].astype(BF16)


def _p4_call(hm, w, cs, qn, wuq, kvn, wukv):
    tm = 512
    return pl.pallas_call(
        _p4_kernel,
        grid=(NT // tm,),
        in_specs=[pl.BlockSpec((tm, D), lambda i: (i, 0)), _const((D, MQR + MKVR + 2 * MROPE)),
                  pl.BlockSpec((tm, 2 * MROPE), lambda i: (i, 0)),
                  _const((1, MQR)), _const((MQR, MH * MHD)),
                  _const((1, MKVR)), _const((MKVR, MH * (MNOPE + MV)))],
        out_specs=[pl.BlockSpec((MH, tm, MHD), lambda i: (0, i, 0)),
                   pl.BlockSpec((MH, tm, MHD), lambda i: (0, i, 0)),
                   pl.BlockSpec((MH, tm, MV), lambda i: (0, i, 0))],
        out_shape=[jax.ShapeDtypeStruct((MH, NT, MHD), BF16),
                   jax.ShapeDtypeStruct((MH, NT, MHD), BF16),
                   jax.ShapeDtypeStruct((MH, NT, MV), BF16)],
        compiler_params=_cparams(("arbitrary",)),
        name="proj_mla",
    )(hm, w, cs, qn, wuq, kvn, wukv)


GLA_TILE = 256
GLA_NCH = GLA_TILE // GCHUNK


def _gla_dir(qk_ref, v_ref, g_ref, o_ref, s_ref, backward):
    g = g_ref[...]
    ri = lax.broadcasted_iota(jnp.int32, (GLA_TILE, GLA_TILE), 0)
    ci = lax.broadcasted_iota(jnp.int32, (GLA_TILE, GLA_TILE), 1)
    same = (ri // GCHUNK) == (ci // GCHUNK)
    tri = jnp.where(same & ((ci >= ri) if backward else (ci <= ri)), 1.0, 0.0).astype(BF16)
    g_hi = g.astype(BF16)
    g_lo = (g - g_hi.astype(F32)).astype(BF16)
    bc = _dot(tri, g_hi) + _dot(tri, g_lo)
    r64 = lax.broadcasted_iota(jnp.int32, (GCHUNK, GCHUNK), 0)
    c64 = lax.broadcasted_iota(jnp.int32, (GCHUNK, GCHUNK), 1)
    mask = (c64 >= r64) if backward else (c64 <= r64)
    order = range(GLA_NCH - 1, -1, -1) if backward else range(GLA_NCH)
    for c in order:
        r0 = c * GCHUNK
        for h in range(GH):
            b = bc[r0:r0 + GCHUNK, h * GDK:(h + 1) * GDK]
            bl = b[0:1] if backward else b[GCHUNK - 1:GCHUNK]
            q = qk_ref[r0:r0 + GCHUNK, h * GDK:(h + 1) * GDK].astype(F32)
            k = qk_ref[r0:r0 + GCHUNK, GKEY + h * GDK:GKEY + (h + 1) * GDK].astype(F32)
            v = v_ref[r0:r0 + GCHUNK, h * GDV:(h + 1) * GDV]
            qe = (q * jnp.exp(b)).astype(BF16)
            ke = (k * jnp.exp(-b)).astype(BF16)
            kd = (k * jnp.exp(bl - b)).astype(BF16)
            a = jnp.where(mask, _dot_nt(qe, ke), 0.0).astype(BF16)
            s = s_ref[h]
            o_ref[r0:r0 + GCHUNK, h * GDV:(h + 1) * GDV] = _dot(a, v) + _dot_nt(qe, s.astype(BF16))
            s_ref[h] = s * jnp.exp(bl) + _dot_tn(v, kd)


def _gla_kernel(qk_ref, v_ref, g_ref, o_ref, s_ref):
    @pl.when(pl.program_id(2) == 0)
    def _():
        s_ref[...] = jnp.zeros_like(s_ref)

    @pl.when(pl.program_id(1) == 0)
    def _():
        _gla_dir(qk_ref, v_ref, g_ref, o_ref, s_ref, backward=False)

    @pl.when(pl.program_id(1) == 1)
    def _():
        _gla_dir(qk_ref, v_ref, g_ref, o_ref, s_ref, backward=True)


def _gla_call(qk, v, lfb):
    nxt = L // GLA_TILE

    def blk(b, d, t):
        xt = jnp.where(d == 0, t - 1, nxt - t)
        return jnp.where(t == 0, NX // GLA_TILE + b, b * nxt + xt)

    return pl.pallas_call(
        _gla_kernel,
        grid=(B, 2, 1 + nxt),
        in_specs=[pl.BlockSpec((GLA_TILE, 2 * GKEY), lambda b, d, t: (blk(b, d, t), 0)),
                  pl.BlockSpec((GLA_TILE, GVAL), lambda b, d, t: (blk(b, d, t), 0)),
                  pl.BlockSpec((GLA_TILE, GKEY), lambda b, d, t: (blk(b, d, t), d))],
        out_specs=pl.BlockSpec((None, GLA_TILE, GVAL), lambda b, d, t: (d, blk(b, d, t), 0)),
        out_shape=jax.ShapeDtypeStruct((2, NT, GVAL), F32),
        scratch_shapes=[pltpu.VMEM((GH, GDV, GDK), F32)],
        compiler_params=_cparams(("arbitrary", "arbitrary", "arbitrary")),
        name="gla",
    )(qk, v, lfb)


ATT_TQ = 256


def _softmax_pv(scores, values):
    m = functools.reduce(jnp.maximum, [jnp.max(s, axis=-1, keepdims=True) for s in scores])
    ps = [jnp.exp(s - m) for s in scores]
    den = functools.reduce(jnp.add, [jnp.sum(p, axis=-1, keepdims=True) for p in ps])
    num = functools.reduce(jnp.add, [_dot(p.astype(BF16), v) for p, v in zip(ps, values)])
    return num / den


def _attn_kernel(q_ref, kx_ref, kc_ref, vx_ref, vc_ref, o_ref, *, nq):
    q = q_ref[...]
    sc = _dot_nt(q, kc_ref[...])

    @pl.when(pl.program_id(2) < nq)
    def _():
        sx = _dot_nt(q, kx_ref[...])
        o_ref[...] = _softmax_pv([sc, sx], [vc_ref[...], vx_ref[...]]).astype(BF16)

    @pl.when(pl.program_id(2) == nq)
    def _():
        o_ref[...] = _softmax_pv([sc], [vc_ref[...]]).astype(BF16)


def _attn_call(qa, ka, va, with_ctx):
    nq = L // ATT_TQ
    rows = NT if with_ctx else NX

    def qblk(b, i):
        return jnp.where(i < nq, b * nq + i, NX // ATT_TQ + b)

    return pl.pallas_call(
        functools.partial(_attn_kernel, nq=nq),
        grid=(B, MH, nq + (1 if with_ctx else 0)),
        in_specs=[pl.BlockSpec((None, ATT_TQ, MHD), lambda b, h, i: (h, qblk(b, i), 0)),
                  pl.BlockSpec((None, L, MHD), lambda b, h, i: (h, b, 0)),
                  pl.BlockSpec((None, LC, MHD), lambda b, h, i: (h, NX // LC + b, 0)),
                  pl.BlockSpec((None, L, MV), lambda b, h, i: (h, b, 0)),
                  pl.BlockSpec((None, LC, MV), lambda b, h, i: (h, NX // LC + b, 0))],
        out_specs=pl.BlockSpec((ATT_TQ, MV), lambda b, h, i: (qblk(b, i), h)),
        out_shape=jax.ShapeDtypeStruct((rows, MH * MV), BF16),
        compiler_params=_cparams(("arbitrary", "arbitrary", "arbitrary")),
        name="attn",
    )(qa, ka, ka, va, va)


MERGE_TM = 512


def _merge_kernel(x_ref, of_ref, ob_ref, r_ref, gn_ref, zb_ref, u_ref, up_ref, un_ref, cw_ref,
                  c_ref, gt_ref, wa_ref, wb_ref, wc_ref, wo_ref, g5_ref, lw_ref, lb_ref, o_ref):
    tm = MERGE_TM
    o = of_ref[...] + ob_ref[...]
    parts = []
    for h in range(GH):
        oh = o[:, h * GDV:(h + 1) * GDV]
        parts.append(oh * lax.rsqrt(jnp.mean(oh * oh, axis=-1, keepdims=True) + EPS))
    on = jnp.concatenate(parts, axis=1) * gn_ref[...]
    a = _dot((on * r_ref[...].astype(F32)).astype(BF16), wa_ref[...])
    u = u_ref[...].astype(F32)
    prev = up_ref[7:8, :].astype(F32)
    nxt = un_ref[0:1, :].astype(F32)
    loc = lax.broadcasted_iota(jnp.int32, (tm, 1), 0)
    row = loc + pl.program_id(0) * tm
    seq = jnp.where(row >= NX, LC, L)
    pos = row & (seq - 1)
    ul = jnp.where(loc == 0, prev, pltpu.roll(u, 1, 0))
    ul = jnp.where(pos == 0, 0.0, ul)
    ur = jnp.where(loc == tm - 1, nxt, pltpu.roll(u, tm - 1, 0))
    ur = jnp.where(pos == seq - 1, 0.0, ur)
    cv = ul * cw_ref[0:1, :] + u * cw_ref[1:2, :] + ur * cw_ref[2:3, :]
    bb = _dot((zb_ref[...].astype(F32) * cv).astype(BF16), wb_ref[...])
    c = _dot(c_ref[...], wc_ref[...])
    mix = (gt_ref[:, :D].astype(F32) * a + gt_ref[:, D:2 * D].astype(F32) * bb
           + gt_ref[:, 2 * D:].astype(F32) * c)
    mx = _dot(mix.astype(BF16), wo_ref[...])
    z = ALPHA * x_ref[...] + g5_ref[...] * mx
    o_ref[...] = _layer_norm(z, lw_ref[...], lb_ref[...])


def _merge_call(x1, o_gla, r, gnorm, zb, u, conv_w, c_pre, gates, wa, wb, wc, wo, g5, lw, lb, rows):
    tm = MERGE_TM
    row = lambda w: pl.BlockSpec((tm, w), lambda i: (i, 0))
    last8 = rows // 8 - 1
    return pl.pallas_call(
        _merge_kernel,
        grid=(rows // tm,),
        in_specs=[row(D),
                  pl.BlockSpec((None, tm, GVAL), lambda i: (0, i, 0)),
                  pl.BlockSpec((None, tm, GVAL), lambda i: (1, i, 0)),
                  row(GVAL), _const((1, GVAL)),
                  row(CW), row(CW),
                  pl.BlockSpec((8, CW), lambda i: (jnp.maximum(i * (tm // 8) - 1, 0), 0)),
                  pl.BlockSpec((8, CW), lambda i: (jnp.minimum((i + 1) * (tm // 8), last8), 0)),
                  _const((3, CW)),
                  row(MH * MV), row(3 * D),
                  _const((GVAL, D)), _const((CW, D)), _const((MH * MV, D)), _const((D, D)),
                  _mod_spec(tm), _const((1, D)), _const((1, D))],
        out_specs=row(D),
        out_shape=jax.ShapeDtypeStruct((rows, D), F32),
        compiler_params=_cparams(("arbitrary",)),
        name="merge",
    )(x1, o_gla, o_gla, r, gnorm, zb, u, u, u, conv_w, c_pre, gates, wa, wb, wc, wo, g5, lw, lb)


def _rope_tables():
    t = jnp.arange(L, dtype=jnp.int32)
    quarter = MROPE // 4
    inv_freq = THETA ** (-jnp.arange(quarter, dtype=F32) / quarter)

    def cs(pos):
        ang = pos.astype(F32)[:, None] * inv_freq
        return jnp.cos(ang), jnp.sin(ang)

    cr, sr = cs(t // GRID_W)
    cc, sn = cs(t % GRID_W)
    tab = jnp.concatenate([cr, cr, cc, cc, -sr, sr, -sn, sn], axis=1)
    ident = jnp.concatenate([jnp.ones((NC, MROPE), F32), jnp.zeros((NC, MROPE), F32)], axis=1)
    return jnp.concatenate([jnp.tile(tab, (B, 1)), ident], axis=0)


def _swap_rope_cols(w):
    q = MROPE // 4
    return jnp.concatenate([w[..., q:2 * q], w[..., :q], w[..., 3 * q:], w[..., 2 * q:3 * q]], axis=-1)


def _layer_weights(l, w_in, gla_decay_w, gla_decay_b, mla_w_uq, mla_w_ukv):
    wi = w_in[l]
    o = 0
    cols = {}
    for name, n in (("q", GKEY), ("k", GKEY), ("v", GVAL), ("r", GVAL), ("gf", GRANK), ("gb", GRANK),
                    ("cb", CW), ("cc", CW), ("cx", CW), ("cq", MQR), ("ckv", MKVR), ("kr", MROPE),
                    ("zg", 3 * D)):
        cols[name] = wi[:, o:o + n]
        o += n
    pad = jnp.zeros((D, 128 - 2 * GRANK), F32)
    w1 = jnp.concatenate([cols["q"], cols["k"], cols["v"], cols["r"], cols["gf"], cols["gb"], pad], axis=1)
    w2 = jnp.concatenate([cols["cb"], cols["cc"], cols["cx"]], axis=1)
    w4 = jnp.concatenate([cols["cq"], cols["ckv"], cols["kr"], _swap_rope_cols(cols["kr"])], axis=1)
    dw = jnp.zeros((128, 2 * GKEY), F32)
    dw = dw.at[:GRANK, :GKEY].set(gla_decay_w[l, 0]).at[GRANK:2 * GRANK, GKEY:].set(gla_decay_w[l, 1])
    db = jnp.concatenate([gla_decay_b[l, 0], gla_decay_b[l, 1]])[None, :]
    wq = mla_w_uq[l].reshape(MQR, MH, MNOPE + MROPE)
    wq = jnp.concatenate([wq, _swap_rope_cols(wq[..., MNOPE:])], axis=-1).reshape(MQR, MH * MHD)
    wkv = mla_w_ukv[l].reshape(MKVR, MH, MNOPE + MV)
    wkv = jnp.concatenate([wkv[..., :MNOPE].reshape(MKVR, MH * MNOPE),
                           wkv[..., MNOPE:].reshape(MKVR, MH * MV)], axis=1)
    bf = lambda a: a.astype(BF16)
    return dict(w1=bf(w1), w2=bf(w2), w3=bf(cols["zg"]), w4=bf(w4), dw=bf(dw), db=db, wq=bf(wq), wkv=bf(wkv))


def kernel(x, c, ctx, c_ctx, ada_w, ada_b, ln_w, ln_b, ffn_w13, ffn_w2, w_in, b_gate, gla_decay_w,
           gla_decay_b, gla_norm, gla_proj, conv_w, conv_proj, mla_q_norm, mla_w_uq, mla_kv_norm,
           mla_w_ukv, mla_proj, w_out):
    assert x.shape == (B, L, D) and ctx.shape == (B, LC, D)
    c_all = jnp.concatenate([c, c_ctx[None, :], jnp.zeros((MODROWS - B - 1, D), F32)], axis=0)
    mods = _ada_call(c_all, ada_w, ada_b).reshape(DEPTH, MODROWS, NMOD, 1, D)
    cs = _rope_tables()
    h = jnp.concatenate([x.reshape(NX, D), ctx.reshape(NC, D)], axis=0)
    vec = lambda a: a.reshape(1, -1)
    for l in range(DEPTH):
        last = l == DEPTH - 1
        rows = NX if last else NT
        m = [mods[l, :, k] for k in range(NMOD)]
        lw = _layer_weights(l, w_in, gla_decay_w, gla_decay_b, mla_w_uq, mla_w_ukv)
        x1, hm = _ffn_call(h, NT, m[0:3], ffn_w13[l, 0].astype(BF16), ffn_w2[l, 0].astype(BF16),
                           vec(ln_w[l, 0]), vec(ln_b[l, 0]), mod_next=(m[3], m[4]))
        qk, v, r, lfb = _p1_call(hm, lw["w1"], lw["dw"], lw["db"])
        zb, u = _p2_call(hm, lw["w2"], rows)
        gates = _p3_call(hm, lw["w3"], vec(b_gate[l]), rows)
        qa, ka, va = _p4_call(hm, lw["w4"], cs, vec(mla_q_norm[l]), lw["wq"], vec(mla_kv_norm[l]), lw["wkv"])
        o_gla = _gla_call(qk, v, lfb)
        c_pre = _attn_call(qa, ka, va, with_ctx=not last)
        x2 = _merge_call(x1, o_gla, r, vec(gla_norm[l]), zb, u, conv_w[l], c_pre, gates,
                         gla_proj[l].astype(BF16), conv_proj[l].astype(BF16), mla_proj[l].astype(BF16),
                         w_out[l].astype(BF16), m[5], vec(ln_w[l, 1]), vec(ln_b[l, 1]), rows)
        h = _ffn_call(x2, rows, m[6:9], ffn_w13[l, 1].astype(BF16), ffn_w2[l, 1].astype(BF16),
                      vec(ln_w[l, 2]), vec(ln_b[l, 2]))
    return h.reshape(B, L, D)
```

```python
import functools

import jax
import jax.numpy as jnp
from jax import lax
from jax.experimental import pallas as pl
from jax.experimental.pallas import tpu as pltpu

F32 = jnp.float32
BF16 = jnp.bfloat16

D = 1024
B = 8
L = 4096
DEPTH = 4
LC = 256
GRID_W = 64
DFF = 2816
GH, GDK, GDV = 4, 128, 256
GKEY, GVAL = GH * GDK, GH * GDV
GRANK = 16
GTEMP = 16.0
GCHUNK = 64
CW = 1024
MH, MQR, MKVR, MNOPE, MROPE, MV = 8, 512, 256, 128, 64, 128
THETA = 10000.0
NMOD = 9
EPS = 1e-6
ALPHA = (2.0 * DEPTH) ** 0.25

NX = B * L
NC = B * LC
NT = NX + NC
MODROWS = 16
MHD = 256

VMEM_LIMIT = 56 * 1024 * 1024


def _cparams(sem):
    return pltpu.CompilerParams(dimension_semantics=sem, vmem_limit_bytes=VMEM_LIMIT)


def _const(shape):
    n = len(shape)
    return pl.BlockSpec(shape, lambda *_: (0,) * n, pipeline_mode=pl.Buffered(1))


def _mod_spec(tm):
    per = L // tm
    return pl.BlockSpec((None, 1, D), lambda i: (jnp.minimum(i // per, B), 0, 0))


def _silu(x):
    return x * jax.nn.sigmoid(x)


def _layer_norm(z, w, b):
    mu = jnp.mean(z, axis=-1, keepdims=True)
    zc = z - mu
    var = jnp.mean(zc * zc, axis=-1, keepdims=True)
    return zc * lax.rsqrt(var + EPS) * w + b


def _rms(x, w):
    return x * lax.rsqrt(jnp.mean(x * x, axis=-1, keepdims=True) + EPS) * w


def _dot(a, b):
    return jnp.dot(a, b, preferred_element_type=F32)


def _dot_nt(a, b):
    return lax.dot_general(a, b, (((1,), (1,)), ((), ())), preferred_element_type=F32)


def _dot_tn(a, b):
    return lax.dot_general(a, b, (((0,), (0,)), ((), ())), preferred_element_type=F32)


def _ada_kernel(c_ref, w_ref, b_ref, o_ref):
    sc = _silu(c_ref[...]).astype(BF16)
    o_ref[...] = _dot(sc, w_ref[...].astype(BF16)) + b_ref[...]


def _ada_call(c_all, ada_w, ada_b):
    tn = 1152
    return pl.pallas_call(
        _ada_kernel,
        grid=(DEPTH, NMOD * D // tn),
        in_specs=[
            pl.BlockSpec((MODROWS, D), lambda l, j: (0, 0)),
            pl.BlockSpec((None, D, tn), lambda l, j: (l, 0, j)),
            pl.BlockSpec((None, 1, tn), lambda l, j: (l, 0, j)),
        ],
        out_specs=pl.BlockSpec((None, MODROWS, tn), lambda l, j: (l, 0, j)),
        out_shape=jax.ShapeDtypeStruct((DEPTH, MODROWS, NMOD * D), F32),
        compiler_params=_cparams(("arbitrary", "arbitrary")),
        name="ada",
    )(c_all, ada_w, ada_b.reshape(DEPTH, 1, NMOD * D))


def _ffn_kernel(x_ref, sh_ref, sc_ref, g_ref, w13_ref, w2_ref, lw_ref, lb_ref, *rest, emit_mod):
    x = x_ref[...]
    hm = (x * (1.0 + sc_ref[...]) + sh_ref[...]).astype(BF16)
    a = _dot(hm, w13_ref[:, :DFF])
    g = _dot(hm, w13_ref[:, DFF:])
    act = (_silu(a) * g).astype(BF16)
    y = _dot(act, w2_ref[...])
    z = ALPHA * x + (0.5 * g_ref[...]) * y
    out = _layer_norm(z, lw_ref[...], lb_ref[...])
    if emit_mod:
        sh2_ref, sc2_ref, o_ref, om_ref = rest
        o_ref[...] = out
        om_ref[...] = (out * (1.0 + sc2_ref[...]) + sh2_ref[...]).astype(BF16)
    else:
        (o_ref,) = rest
        o_ref[...] = out


def _ffn_call(h, rows, mod3, w13, w2, lw, lb, mod_next=None):
    tm = 512
    emit = mod_next is not None
    row = pl.BlockSpec((tm, D), lambda i: (i, 0))
    in_specs = [row, _mod_spec(tm), _mod_spec(tm), _mod_spec(tm),
                _const((D, 2 * DFF)), _const((DFF, D)), _const((1, D)), _const((1, D))]
    args = [h, *mod3, w13, w2, lw, lb]
    out_shape = [jax.ShapeDtypeStruct((rows, D), F32)]
    out_specs = [row]
    if emit:
        in_specs += [_mod_spec(tm), _mod_spec(tm)]
        args += list(mod_next)
        out_shape.append(jax.ShapeDtypeStruct((rows, D), BF16))
        out_specs.append(row)
    res = pl.pallas_call(
        functools.partial(_ffn_kernel, emit_mod=emit),
        grid=(rows // tm,),
        in_specs=in_specs,
        out_specs=out_specs,
        out_shape=out_shape,
        compiler_params=_cparams(("arbitrary",)),
        name="ffn",
    )(*args)
    return res if emit else res[0]


def _p1_kernel(hm_ref, w_ref, wg_ref, dw_ref, db_ref, qk_ref, v_ref, r_ref, lfb_ref):
    hm = hm_ref[...]
    glr = _dot(hm, wg_ref[...]).astype(BF16)
    xg = _dot(glr, dw_ref[...]) + db_ref[...]
    lfb_ref[...] = (jnp.minimum(xg, 0.0) - jnp.log(1.0 + jnp.exp(-jnp.abs(xg)))) * (1.0 / GTEMP)
    z = _dot(hm, w_ref[...])
    qk_ref[:, :GKEY] = (z[:, :GKEY] * (GDK ** -0.5)).astype(BF16)
    qk_ref[:, GKEY:] = z[:, GKEY:2 * GKEY].astype(BF16)
    v_ref[...] = z[:, 2 * GKEY:2 * GKEY + GVAL].astype(BF16)
    r_ref[...] = _silu(z[:, 2 * GKEY + GVAL:]).astype(BF16)


def _p1_call(hm, w, wg, dw, db):
    tm = 512
    return pl.pallas_call(
        _p1_kernel,
        grid=(NT // tm,),
        in_specs=[pl.BlockSpec((tm, D), lambda i: (i, 0)),
                  _const((D, 2 * GKEY + 2 * GVAL)), _const((D, 128)),
                  _const((128, 2 * GKEY)), _const((1, 2 * GKEY))],
        out_specs=[pl.BlockSpec((tm, 2 * GKEY), lambda i: (i, 0)),
                   pl.BlockSpec((tm, GVAL), lambda i: (i, 0)),
                   pl.BlockSpec((tm, GVAL), lambda i: (i, 0)),
                   pl.BlockSpec((tm, 2 * GKEY), lambda i: (i, 0))],
        out_shape=[jax.ShapeDtypeStruct((NT, 2 * GKEY), BF16),
                   jax.ShapeDtypeStruct((NT, GVAL), BF16),
                   jax.ShapeDtypeStruct((NT, GVAL), BF16),
                   jax.ShapeDtypeStruct((NT, 2 * GKEY), F32)],
        compiler_params=_cparams(("arbitrary",)),
        name="proj_gla",
    )(hm, w, wg, dw, db)


def _p2_kernel(hm_ref, w_ref, zb_ref, u_ref):
    z = _dot(hm_ref[...], w_ref[...])
    zb_ref[...] = z[:, :CW].astype(BF16)
    u_ref[...] = (z[:, CW:2 * CW] * z[:, 2 * CW:]).astype(BF16)


def _p2_call(hm, w, rows):
    tm = 512
    return pl.pallas_call(
        _p2_kernel,
        grid=(rows // tm,),
        in_specs=[pl.BlockSpec((tm, D), lambda i: (i, 0)), _const((D, 3 * CW))],
        out_specs=[pl.BlockSpec((tm, CW), lambda i: (i, 0)),
                   pl.BlockSpec((tm, CW), lambda i: (i, 0))],
        out_shape=[jax.ShapeDtypeStruct((rows, CW), BF16),
                   jax.ShapeDtypeStruct((rows, CW), BF16)],
        compiler_params=_cparams(("arbitrary",)),
        name="proj_conv",
    )(hm, w)


def _p3_kernel(hm_ref, w_ref, b_ref, g_ref):
    z = _dot(hm_ref[...], w_ref[...]) + b_ref[...]
    g_ref[...] = jax.nn.sigmoid(z).astype(BF16)


def _p3_call(hm, w, b_gate, rows):
    tm = 512
    return pl.pallas_call(
        _p3_kernel,
        grid=(rows // tm,),
        in_specs=[pl.BlockSpec((tm, D), lambda i: (i, 0)), _const((D, 3 * D)), _const((1, 3 * D))],
        out_specs=pl.BlockSpec((tm, 3 * D), lambda i: (i, 0)),
        out_shape=jax.ShapeDtypeStruct((rows, 3 * D), BF16),
        compiler_params=_cparams(("arbitrary",)),
        name="proj_gate",
    )(hm, w, b_gate)


def _rope_fold(t, cs):
    rr = t * cs
    return rr + pltpu.roll(rr, MROPE, 1)


Q_SCALE = (MNOPE + MROPE) ** -0.5 * 1.4426950408889634


def _p4_kernel(hm_ref, w_ref, cs_ref, qn_ref, wuq_ref, kvn_ref, wuk_ref, wuvt_ref, q_ref, k_ref, vt_ref):
    z = _dot(hm_ref[...], w_ref[...])
    cs = cs_ref[...]
    cqn = _rms(z[:, :MQR], qn_ref[...]).astype(BF16)
    qz = _dot(cqn, wuq_ref[...]) * Q_SCALE
    for h in range(MH):
        base = h * MHD
        rot = _rope_fold(qz[:, base + MNOPE:base + MHD], cs)
        q_ref[h] = jnp.concatenate([qz[:, base:base + MNOPE], rot], axis=1).astype(BF16)
    kvn = _rms(z[:, MQR:MQR + MKVR], kvn_ref[...]).astype(BF16)
    kz = _dot(kvn, wuk_ref[...])
    krot = _rope_fold(z[:, MQR + MKVR:], cs)
    lane = lax.broadcasted_iota(jnp.int32, krot.shape, 1)
    krot = jnp.where(lane < MROPE, krot, 0.0)
    for h in range(MH):
        k_ref[h] = jnp.concatenate([kz[:, h * MNOPE:(h + 1) * MNOPE], krot], axis=1).astype(BF16)
    vt = _dot_nt(wuvt_ref[...], kvn)
    for h in range(MH):
        vt_ref[h] = vt[h * MV:(h + 1) * MV, :].astype(BF16)


def _p4_call(hm, w, cs, qn, wuq, kvn, wuk, wuvt):
    tm = 512
    return pl.pallas_call(
        _p4_kernel,
        grid=(NT // tm,),
        in_specs=[pl.BlockSpec((tm, D), lambda i: (i, 0)), _const((D, MQR + MKVR + 2 * MROPE)),
                  pl.BlockSpec((tm, 2 * MROPE), lambda i: (i, 0)),
                  _const((1, MQR)), _const((MQR, MH * MHD)),
                  _const((1, MKVR)), _const((MKVR, MH * MNOPE)), _const((MH * MV, MKVR))],
        out_specs=[pl.BlockSpec((MH, tm, MHD), lambda i: (0, i, 0)),
                   pl.BlockSpec((MH, tm, MHD), lambda i: (0, i, 0)),
                   pl.BlockSpec((MH, MV, tm), lambda i: (0, 0, i))],
        out_shape=[jax.ShapeDtypeStruct((MH, NT, MHD), BF16),
                   jax.ShapeDtypeStruct((MH, NT, MHD), BF16),
                   jax.ShapeDtypeStruct((MH, MV, NT), BF16)],
        compiler_params=_cparams(("arbitrary",)),
        name="proj_mla",
    )(hm, w, cs, qn, wuq, kvn, wuk, wuvt)


GLA_TILE = 256
GLA_NCH = GLA_TILE // GCHUNK


def _gla_local(qk_ref, v_ref, g_ref, backward):
    g = g_ref[...]
    ri = lax.broadcasted_iota(jnp.int32, (GLA_TILE, GLA_TILE), 0)
    ci = lax.broadcasted_iota(jnp.int32, (GLA_TILE, GLA_TILE), 1)
    same = (ri // GCHUNK) == (ci // GCHUNK)
    tri = jnp.where(same & ((ci >= ri) if backward else (ci <= ri)), 1.0, 0.0).astype(BF16)
    g_hi = g.astype(BF16)
    g_lo = (g - g_hi.astype(F32)).astype(BF16)
    bc = _dot(tri, g_hi) + _dot(tri, g_lo)
    r64 = lax.broadcasted_iota(jnp.int32, (GCHUNK, GCHUNK), 0)
    c64 = lax.broadcasted_iota(jnp.int32, (GCHUNK, GCHUNK), 1)
    mask = (c64 >= r64) if backward else (c64 <= r64)
    order = range(GLA_NCH - 1, -1, -1) if backward else range(GLA_NCH)
    items = []
    for c in order:
        r0 = c * GCHUNK
        for h in range(GH):
            b = bc[r0:r0 + GCHUNK, h * GDK:(h + 1) * GDK]
            bl = b[0:1] if backward else b[GCHUNK - 1:GCHUNK]
            q = qk_ref[r0:r0 + GCHUNK, h * GDK:(h + 1) * GDK].astype(F32)
            k = qk_ref[r0:r0 + GCHUNK, GKEY + h * GDK:GKEY + (h + 1) * GDK].astype(F32)
            qe = (q * jnp.exp(b)).astype(BF16)
            ke = (k * jnp.exp(-b)).astype(BF16)
            kd = (k * jnp.exp(bl - b)).astype(BF16)
            a = jnp.where(mask, _dot_nt(qe, ke), 0.0).astype(BF16)
            items.append(dict(r0=r0, h=h, qe=qe, kd=kd, a=a, dec=jnp.exp(bl)))
    for it in items:
        v = v_ref[it["r0"]:it["r0"] + GCHUNK, it["h"] * GDV:(it["h"] + 1) * GDV]
        it["oi"] = _dot(it.pop("a"), v)
        it["u"] = _dot_tn(v, it.pop("kd"))
    return items


def _gla_kernel(qkf_ref, vf_ref, gf_ref, qkb_ref, vb_ref, gb_ref, of_ref, ob_ref, s_ref):
    @pl.when(pl.program_id(1) == 0)
    def _():
        s_ref[...] = jnp.zeros_like(s_ref)

    scans = [(_gla_local(qkf_ref, vf_ref, gf_ref, False), of_ref, 0),
             (_gla_local(qkb_ref, vb_ref, gb_ref, True), ob_ref, 1)]
    state = [[s_ref[d, h] for h in range(GH)] for d in range(2)]
    for n in range(GLA_NCH * GH):
        for items, o_ref, d in scans:
            it = items[n]
            r0, h = it["r0"], it["h"]
            s = state[d][h]
            o_ref[r0:r0 + GCHUNK, h * GDV:(h + 1) * GDV] = it["oi"] + _dot_nt(it["qe"], s.astype(BF16))
            state[d][h] = s * it["dec"] + it["u"]
    for d in range(2):
        for h in range(GH):
            s_ref[d, h] = state[d][h]


def _gla_call(qk, v, lfb):
    nxt = L // GLA_TILE

    def fwd(b, t):
        return jnp.where(t == 0, NX // GLA_TILE + b, b * nxt + t - 1)

    def bwd(b, t):
        return jnp.where(t == 0, NX // GLA_TILE + b, b * nxt + nxt - t)

    def specs(blk, d):
        return [pl.BlockSpec((GLA_TILE, 2 * GKEY), lambda b, t: (blk(b, t), 0)),
                pl.BlockSpec((GLA_TILE, GVAL), lambda b, t: (blk(b, t), 0)),
                pl.BlockSpec((GLA_TILE, GKEY), lambda b, t: (blk(b, t), d))]

    return pl.pallas_call(
        _gla_kernel,
        grid=(B, 1 + nxt),
        in_specs=specs(fwd, 0) + specs(bwd, 1),
        out_specs=[pl.BlockSpec((GLA_TILE, GVAL), lambda b, t: (fwd(b, t), 0)),
                   pl.BlockSpec((GLA_TILE, GVAL), lambda b, t: (bwd(b, t), 0))],
        out_shape=[jax.ShapeDtypeStruct((NT, GVAL), F32), jax.ShapeDtypeStruct((NT, GVAL), F32)],
        scratch_shapes=[pltpu.VMEM((2, GH, GDV, GDK), F32)],
        compiler_params=_cparams(("arbitrary", "arbitrary")),
        name="gla",
    )(qk, v, lfb, qk, v, lfb)


ATT_TQ = 512
ATT_KC = 512
ATT_AHEAD = 2


def _flash_t(q, chunks):
    m = l = acc = None
    ahead = [_dot_nt(chunks[j][0], q) for j in range(min(ATT_AHEAD, len(chunks)))]
    for j, (_, vt) in enumerate(chunks):
        s = ahead.pop(0)
        if j + ATT_AHEAD < len(chunks):
            ahead.append(_dot_nt(chunks[j + ATT_AHEAD][0], q))
        cm = jnp.max(s, axis=0, keepdims=True)
        if m is None:
            m = cm
            p = jnp.exp2(s - m)
            l = jnp.sum(p, axis=0, keepdims=True)
            acc = _dot(vt, p.astype(BF16))
        else:
            m_new = jnp.maximum(m, cm)
            alpha = jnp.exp2(m - m_new)
            p = jnp.exp2(s - m_new)
            l = alpha * l + jnp.sum(p, axis=0, keepdims=True)
            acc = alpha * acc + _dot(vt, p.astype(BF16))
            m = m_new
    return acc / l


def _attn_kernel(q_ref, kx_ref, kc_ref, vtx_ref, vtc_ref, o_ref):
    chunks = [(kc_ref[...], vtc_ref[...])]
    for j in range(L // ATT_KC):
        chunks.append((kx_ref[j * ATT_KC:(j + 1) * ATT_KC, :], vtx_ref[:, j * ATT_KC:(j + 1) * ATT_KC]))
    o_ref[...] = _flash_t(q_ref[...], chunks).T.astype(BF16)


def _attn_ctx_kernel(q_ref, kc_ref, vtc_ref, prev_ref, o_ref):
    del prev_ref
    o_ref[...] = _flash_t(q_ref[...], [(kc_ref[...], vtc_ref[...])]).T.astype(BF16)


def _attn_call(qa, ka, vta, with_ctx):
    nq = L // ATT_TQ
    out = pl.pallas_call(
        _attn_kernel,
        grid=(B, MH, nq),
        in_specs=[pl.BlockSpec((None, ATT_TQ, MHD), lambda b, h, i: (h, b * nq + i, 0)),
                  pl.BlockSpec((None, L, MHD), lambda b, h, i: (h, b, 0)),
                  pl.BlockSpec((None, LC, MHD), lambda b, h, i: (h, NX // LC + b, 0)),
                  pl.BlockSpec((None, MV, L), lambda b, h, i: (h, 0, b)),
                  pl.BlockSpec((None, MV, LC), lambda b, h, i: (h, 0, NX // LC + b))],
        out_specs=pl.BlockSpec((ATT_TQ, MV), lambda b, h, i: (b * nq + i, h)),
        out_shape=jax.ShapeDtypeStruct((NT if with_ctx else NX, MH * MV), BF16),
        compiler_params=_cparams(("arbitrary", "arbitrary", "arbitrary")),
        name="attn",
    )(qa, ka, ka, vta, vta)
    if not with_ctx:
        return out
    return pl.pallas_call(
        _attn_ctx_kernel,
        grid=(B, MH),
        in_specs=[pl.BlockSpec((None, LC, MHD), lambda b, h: (h, NX // LC + b, 0)),
                  pl.BlockSpec((None, LC, MHD), lambda b, h: (h, NX // LC + b, 0)),
                  pl.BlockSpec((None, MV, LC), lambda b, h: (h, 0, NX // LC + b)),
                  pl.BlockSpec(memory_space=pl.ANY)],
        out_specs=pl.BlockSpec((LC, MV), lambda b, h: (NX // LC + b, h)),
        out_shape=jax.ShapeDtypeStruct((NT, MH * MV), BF16),
        input_output_aliases={3: 0},
        compiler_params=_cparams(("arbitrary", "arbitrary")),
        name="attn_ctx",
    )(qa, ka, vta, out)


MERGE_TM = 512


def _merge_kernel(x_ref, of_ref, ob_ref, r_ref, gn_ref, zb_ref, u_ref, up_ref, un_ref, cw_ref,
                  c_ref, gt_ref, wa_ref, wb_ref, wc_ref, wo_ref, g5_ref, lw_ref, lb_ref, o_ref):
    tm = MERGE_TM
    o = of_ref[...] + ob_ref[...]
    parts = []
    for h in range(GH):
        oh = o[:, h * GDV:(h + 1) * GDV]
        parts.append(oh * lax.rsqrt(jnp.mean(oh * oh, axis=-1, keepdims=True) + EPS))
    on = jnp.concatenate(parts, axis=1) * gn_ref[...]
    a = _dot((on * r_ref[...].astype(F32)).astype(BF16), wa_ref[...])
    u = u_ref[...].astype(F32)
    prev = up_ref[7:8, :].astype(F32)
    nxt = un_ref[0:1, :].astype(F32)
    loc = lax.broadcasted_iota(jnp.int32, (tm, 1), 0)
    row = loc + pl.program_id(0) * tm
    seq = jnp.where(row >= NX, LC, L)
    pos = row & (seq - 1)
    ul = jnp.where(loc == 0, prev, pltpu.roll(u, 1, 0))
    ul = jnp.where(pos == 0, 0.0, ul)
    ur = jnp.where(loc == tm - 1, nxt, pltpu.roll(u, tm - 1, 0))
    ur = jnp.where(pos == seq - 1, 0.0, ur)
    cv = ul * cw_ref[0:1, :] + u * cw_ref[1:2, :] + ur * cw_ref[2:3, :]
    bb = _dot((zb_ref[...].astype(F32) * cv).astype(BF16), wb_ref[...])
    c = _dot(c_ref[...], wc_ref[...])
    mix = (gt_ref[:, :D].astype(F32) * a + gt_ref[:, D:2 * D].astype(F32) * bb
           + gt_ref[:, 2 * D:].astype(F32) * c)
    mx = _dot(mix.astype(BF16), wo_ref[...])
    z = ALPHA * x_ref[...] + g5_ref[...] * mx
    o_ref[...] = _layer_norm(z, lw_ref[...], lb_ref[...])


def _merge_call(x1, o_f, o_b, r, gnorm, zb, u, conv_w, c_pre, gates, wa, wb, wc, wo, g5, lw, lb, rows):
    tm = MERGE_TM
    row = lambda w: pl.BlockSpec((tm, w), lambda i: (i, 0))
    last8 = rows // 8 - 1
    return pl.pallas_call(
        _merge_kernel,
        grid=(rows // tm,),
        in_specs=[row(D), row(GVAL), row(GVAL), row(GVAL), _const((1, GVAL)),
                  row(CW), row(CW),
                  pl.BlockSpec((8, CW), lambda i: (jnp.maximum(i * (tm // 8) - 1, 0), 0)),
                  pl.BlockSpec((8, CW), lambda i: (jnp.minimum((i + 1) * (tm // 8), last8), 0)),
                  _const((3, CW)),
                  row(MH * MV), row(3 * D),
                  _const((GVAL, D)), _const((CW, D)), _const((MH * MV, D)), _const((D, D)),
                  _mod_spec(tm), _const((1, D)), _const((1, D))],
        out_specs=row(D),
        out_shape=jax.ShapeDtypeStruct((rows, D), F32),
        compiler_params=_cparams(("arbitrary",)),
        name="merge",
    )(x1, o_f, o_b, r, gnorm, zb, u, u, u, conv_w, c_pre, gates, wa, wb, wc, wo, g5, lw, lb)


def _rope_tables():
    t = jnp.arange(L, dtype=jnp.int32)
    quarter = MROPE // 4
    inv_freq = THETA ** (-jnp.arange(quarter, dtype=F32) / quarter)

    def cs(pos):
        ang = pos.astype(F32)[:, None] * inv_freq
        return jnp.cos(ang), jnp.sin(ang)

    cr, sr = cs(t // GRID_W)
    cc, sn = cs(t % GRID_W)
    tab = jnp.concatenate([cr, cr, cc, cc, -sr, sr, -sn, sn], axis=1)
    ident = jnp.concatenate([jnp.ones((NC, MROPE), F32), jnp.zeros((NC, MROPE), F32)], axis=1)
    return jnp.concatenate([jnp.tile(tab, (B, 1)), ident], axis=0)


def _swap_rope_cols(w):
    q = MROPE // 4
    return jnp.concatenate([w[..., q:2 * q], w[..., :q], w[..., 3 * q:], w[..., 2 * q:3 * q]], axis=-1)


def _layer_weights(l, w_in, gla_decay_w, gla_decay_b, mla_w_uq, mla_w_ukv):
    wi = w_in[l]
    o = 0
    cols = {}
    for name, n in (("q", GKEY), ("k", GKEY), ("v", GVAL), ("r", GVAL), ("gf", GRANK), ("gb", GRANK),
                    ("cb", CW), ("cc", CW), ("cx", CW), ("cq", MQR), ("ckv", MKVR), ("kr", MROPE),
                    ("zg", 3 * D)):
        cols[name] = wi[:, o:o + n]
        o += n
    pad = jnp.zeros((D, 128 - 2 * GRANK), F32)
    w1 = jnp.concatenate([cols["q"], cols["k"], cols["v"], cols["r"]], axis=1)
    wg = jnp.concatenate([cols["gf"], cols["gb"], pad], axis=1)
    w2 = jnp.concatenate([cols["cb"], cols["cc"], cols["cx"]], axis=1)
    w4 = jnp.concatenate([cols["cq"], cols["ckv"], cols["kr"], _swap_rope_cols(cols["kr"])], axis=1)
    dw = jnp.zeros((128, 2 * GKEY), F32)
    dw = dw.at[:GRANK, :GKEY].set(gla_decay_w[l, 0]).at[GRANK:2 * GRANK, GKEY:].set(gla_decay_w[l, 1])
    db = jnp.concatenate([gla_decay_b[l, 0], gla_decay_b[l, 1]])[None, :]
    wq = mla_w_uq[l].reshape(MQR, MH, MNOPE + MROPE)
    wq = jnp.concatenate([wq, _swap_rope_cols(wq[..., MNOPE:])], axis=-1).reshape(MQR, MH * MHD)
    wkv = mla_w_ukv[l].reshape(MKVR, MH, MNOPE + MV)
    wk = wkv[..., :MNOPE].reshape(MKVR, MH * MNOPE)
    wvt = wkv[..., MNOPE:].reshape(MKVR, MH * MV).T
    bf = lambda a: a.astype(BF16)
    return dict(w1=bf(w1), wg=bf(wg), w2=bf(w2), w3=bf(cols["zg"]), w4=bf(w4), dw=bf(dw), db=db, wq=bf(wq),
                wk=bf(wk), wvt=bf(wvt))


def kernel(x, c, ctx, c_ctx, ada_w, ada_b, ln_w, ln_b, ffn_w13, ffn_w2, w_in, b_gate, gla_decay_w,
           gla_decay_b, gla_norm, gla_proj, conv_w, conv_proj, mla_q_norm, mla_w_uq, mla_kv_norm,
           mla_w_ukv, mla_proj, w_out):
    assert x.shape == (B, L, D) and ctx.shape == (B, LC, D)
    c_all = jnp.concatenate([c, c_ctx[None, :], jnp.zeros((MODROWS - B - 1, D), F32)], axis=0)
    mods = _ada_call(c_all, ada_w, ada_b).reshape(DEPTH, MODROWS, NMOD, 1, D)
    cs = _rope_tables()
    h = jnp.concatenate([x.reshape(NX, D), ctx.reshape(NC, D)], axis=0)
    vec = lambda a: a.reshape(1, -1)
    for l in range(DEPTH):
        last = l == DEPTH - 1
        rows = NX if last else NT
        m = [mods[l, :, k] for k in range(NMOD)]
        lw = _layer_weights(l, w_in, gla_decay_w, gla_decay_b, mla_w_uq, mla_w_ukv)
        x1, hm = _ffn_call(h, NT, m[0:3], ffn_w13[l, 0].astype(BF16), ffn_w2[l, 0].astype(BF16),
                           vec(ln_w[l, 0]), vec(ln_b[l, 0]), mod_next=(m[3], m[4]))
        qk, v, r, lfb = _p1_call(hm, lw["w1"], lw["wg"], lw["dw"], lw["db"])
        zb, u = _p2_call(hm, lw["w2"], rows)
        gates = _p3_call(hm, lw["w3"], vec(b_gate[l]), rows)
        qa, ka, vta = _p4_call(hm, lw["w4"], cs, vec(mla_q_norm[l]), lw["wq"], vec(mla_kv_norm[l]),
                               lw["wk"], lw["wvt"])
        o_f, o_b = _gla_call(qk, v, lfb)
        c_pre = _attn_call(qa, ka, vta, with_ctx=not last)
        x2 = _merge_call(x1, o_f, o_b, r, vec(gla_norm[l]), zb, u, conv_w[l], c_pre, gates,
                         gla_proj[l].astype(BF16), conv_proj[l].astype(BF16), mla_proj[l].astype(BF16),
                         w_out[l].astype(BF16), m[5], vec(ln_w[l, 1]), vec(ln_b[l, 1]), rows)
        h = _ffn_call(x2, rows, m[6:9], ffn_w13[l, 1].astype(BF16), ffn_w2[l, 1].astype(BF16),
                      vec(ln_w[l, 2]), vec(ln_b[l, 2]))
    return h.reshape(B, L, D)
```

```python
import functools

import jax
import jax.numpy as jnp
from jax import lax
from jax.experimental import pallas as pl
from jax.experimental.pallas import tpu as pltpu

F32 = jnp.float32
BF16 = jnp.bfloat16

D = 1024
B = 8
L = 4096
DEPTH = 4
LC = 256
GRID_W = 64
DFF = 2816
GH, GDK, GDV = 4, 128, 256
GKEY, GVAL = GH * GDK, GH * GDV
GRANK = 16
GTEMP = 16.0
GCHUNK = 64
CW = 1024
MH, MQR, MKVR, MNOPE, MROPE, MV = 8, 512, 256, 128, 64, 128
THETA = 10000.0
NMOD = 9
EPS = 1e-6
ALPHA = (2.0 * DEPTH) ** 0.25

NX = B * L
NC = B * LC
NT = NX + NC
MODROWS = 16
MHD = 256

VMEM_LIMIT = 56 * 1024 * 1024


def _cparams(sem):
    return pltpu.CompilerParams(dimension_semantics=sem, vmem_limit_bytes=VMEM_LIMIT)


def _const(shape):
    n = len(shape)
    return pl.BlockSpec(shape, lambda *_: (0,) * n, pipeline_mode=pl.Buffered(1))


def _mod_spec(tm):
    per = L // tm
    return pl.BlockSpec((None, 1, D), lambda i: (jnp.minimum(i // per, B), 0, 0))


def _silu(x):
    return x * jax.nn.sigmoid(x)


def _layer_norm(z, w, b):
    mu = jnp.mean(z, axis=-1, keepdims=True)
    zc = z - mu
    var = jnp.mean(zc * zc, axis=-1, keepdims=True)
    return zc * lax.rsqrt(var + EPS) * w + b


def _rms(x, w):
    return x * lax.rsqrt(jnp.mean(x * x, axis=-1, keepdims=True) + EPS) * w


def _dot(a, b):
    return jnp.dot(a, b, preferred_element_type=F32)


def _dot_nt(a, b):
    return lax.dot_general(a, b, (((1,), (1,)), ((), ())), preferred_element_type=F32)


def _dot_tn(a, b):
    return lax.dot_general(a, b, (((0,), (0,)), ((), ())), preferred_element_type=F32)


def _ada_kernel(c_ref, w_ref, b_ref, o_ref):
    sc = _silu(c_ref[...]).astype(BF16)
    o_ref[...] = _dot(sc, w_ref[...].astype(BF16)) + b_ref[...]


def _ada_call(c_all, ada_w, ada_b):
    tn = 1152
    return pl.pallas_call(
        _ada_kernel,
        grid=(DEPTH, NMOD * D // tn),
        in_specs=[
            pl.BlockSpec((MODROWS, D), lambda l, j: (0, 0)),
            pl.BlockSpec((None, D, tn), lambda l, j: (l, 0, j)),
            pl.BlockSpec((None, 1, tn), lambda l, j: (l, 0, j)),
        ],
        out_specs=pl.BlockSpec((None, MODROWS, tn), lambda l, j: (l, 0, j)),
        out_shape=jax.ShapeDtypeStruct((DEPTH, MODROWS, NMOD * D), F32),
        compiler_params=_cparams(("arbitrary", "arbitrary")),
        name="ada",
    )(c_all, ada_w, ada_b.reshape(DEPTH, 1, NMOD * D))


SUB_ROWS = 256
PROJ_TM = 1024

def _ffn_kernel(x_ref, sh_ref, sc_ref, g_ref, w13_ref, w2_ref, lw_ref, lb_ref, *rest, emit_mod):
    for r0 in range(0, x_ref.shape[0], SUB_ROWS):
        rs = slice(r0, r0 + SUB_ROWS)
        x = x_ref[rs, :]
        hm = (x * (1.0 + sc_ref[...]) + sh_ref[...]).astype(BF16)
        a = _dot(hm, w13_ref[:, :DFF])
        g = _dot(hm, w13_ref[:, DFF:])
        act = (_silu(a) * g).astype(BF16)
        y = _dot(act, w2_ref[...])
        z = ALPHA * x + (0.5 * g_ref[...]) * y
        out = _layer_norm(z, lw_ref[...], lb_ref[...])
        if emit_mod:
            sh2_ref, sc2_ref, o_ref, om_ref = rest
            o_ref[rs, :] = out
            om_ref[rs, :] = (out * (1.0 + sc2_ref[...]) + sh2_ref[...]).astype(BF16)
        else:
            (o_ref,) = rest
            o_ref[rs, :] = out


def _ffn_call(h, rows, mod3, w13, w2, lw, lb, mod_next=None):
    tm = 1024
    emit = mod_next is not None
    row = pl.BlockSpec((tm, D), lambda i: (i, 0))
    in_specs = [row, _mod_spec(tm), _mod_spec(tm), _mod_spec(tm),
                _const((D, 2 * DFF)), _const((DFF, D)), _const((1, D)), _const((1, D))]
    args = [h, *mod3, w13, w2, lw, lb]
    out_shape = [jax.ShapeDtypeStruct((rows, D), F32)]
    out_specs = [row]
    if emit:
        in_specs += [_mod_spec(tm), _mod_spec(tm)]
        args += list(mod_next)
        out_shape.append(jax.ShapeDtypeStruct((rows, D), BF16))
        out_specs.append(row)
    res = pl.pallas_call(
        functools.partial(_ffn_kernel, emit_mod=emit),
        grid=(rows // tm,),
        in_specs=in_specs,
        out_specs=out_specs,
        out_shape=out_shape,
        compiler_params=_cparams(("arbitrary",)),
        name="ffn",
    )(*args)
    return res if emit else res[0]


def _subtiles(ref):
    return [slice(r0, r0 + SUB_ROWS) for r0 in range(0, ref.shape[0], SUB_ROWS)]


def _p1_kernel(hm_ref, w_ref, wg_ref, dw_ref, db_ref, qk_ref, v_ref, r_ref, lfb_ref):
    for rs in _subtiles(hm_ref):
        hm = hm_ref[rs, :]
        glr = _dot(hm, wg_ref[...]).astype(BF16)
        xg = _dot(glr, dw_ref[...]) + db_ref[...]
        lfb_ref[rs, :] = (jnp.minimum(xg, 0.0) - jnp.log(1.0 + jnp.exp(-jnp.abs(xg)))) * (1.0 / GTEMP)
        z = _dot(hm, w_ref[...])
        qk_ref[rs, :GKEY] = (z[:, :GKEY] * (GDK ** -0.5)).astype(BF16)
        qk_ref[rs, GKEY:] = z[:, GKEY:2 * GKEY].astype(BF16)
        v_ref[rs, :] = z[:, 2 * GKEY:2 * GKEY + GVAL].astype(BF16)
        r_ref[rs, :] = _silu(z[:, 2 * GKEY + GVAL:]).astype(BF16)


def _p1_call(hm, w, wg, dw, db):
    tm = PROJ_TM
    return pl.pallas_call(
        _p1_kernel,
        grid=(NT // tm,),
        in_specs=[pl.BlockSpec((tm, D), lambda i: (i, 0)),
                  _const((D, 2 * GKEY + 2 * GVAL)), _const((D, 128)),
                  _const((128, 2 * GKEY)), _const((1, 2 * GKEY))],
        out_specs=[pl.BlockSpec((tm, 2 * GKEY), lambda i: (i, 0)),
                   pl.BlockSpec((tm, GVAL), lambda i: (i, 0)),
                   pl.BlockSpec((tm, GVAL), lambda i: (i, 0)),
                   pl.BlockSpec((tm, 2 * GKEY), lambda i: (i, 0))],
        out_shape=[jax.ShapeDtypeStruct((NT, 2 * GKEY), BF16),
                   jax.ShapeDtypeStruct((NT, GVAL), BF16),
                   jax.ShapeDtypeStruct((NT, GVAL), BF16),
                   jax.ShapeDtypeStruct((NT, 2 * GKEY), F32)],
        compiler_params=_cparams(("arbitrary",)),
        name="proj_gla",
    )(hm, w, wg, dw, db)


def _p2_kernel(hm_ref, w_ref, zb_ref, u_ref):
    for rs in _subtiles(hm_ref):
        z = _dot(hm_ref[rs, :], w_ref[...])
        zb_ref[rs, :] = z[:, :CW].astype(BF16)
        u_ref[rs, :] = (z[:, CW:2 * CW] * z[:, 2 * CW:]).astype(BF16)


def _p2_call(hm, w, rows):
    tm = PROJ_TM
    return pl.pallas_call(
        _p2_kernel,
        grid=(rows // tm,),
        in_specs=[pl.BlockSpec((tm, D), lambda i: (i, 0)), _const((D, 3 * CW))],
        out_specs=[pl.BlockSpec((tm, CW), lambda i: (i, 0)),
                   pl.BlockSpec((tm, CW), lambda i: (i, 0))],
        out_shape=[jax.ShapeDtypeStruct((rows, CW), BF16),
                   jax.ShapeDtypeStruct((rows, CW), BF16)],
        compiler_params=_cparams(("arbitrary",)),
        name="proj_conv",
    )(hm, w)


def _p3_kernel(hm_ref, w_ref, b_ref, g_ref):
    for rs in _subtiles(hm_ref):
        z = _dot(hm_ref[rs, :], w_ref[...]) + b_ref[...]
        g_ref[rs, :] = jax.nn.sigmoid(z).astype(BF16)


def _p3_call(hm, w, b_gate, rows):
    tm = PROJ_TM
    return pl.pallas_call(
        _p3_kernel,
        grid=(rows // tm,),
        in_specs=[pl.BlockSpec((tm, D), lambda i: (i, 0)), _const((D, 3 * D)), _const((1, 3 * D))],
        out_specs=pl.BlockSpec((tm, 3 * D), lambda i: (i, 0)),
        out_shape=jax.ShapeDtypeStruct((rows, 3 * D), BF16),
        compiler_params=_cparams(("arbitrary",)),
        name="proj_gate",
    )(hm, w, b_gate)


def _rope_fold(t, cs):
    rr = t * cs
    return rr + pltpu.roll(rr, MROPE, 1)


Q_SCALE = (MNOPE + MROPE) ** -0.5 * 1.4426950408889634


def _p4_kernel(hm_ref, w_ref, cs_ref, qn_ref, wuq_ref, kvn_ref, wuk_ref, wuvt_ref, q_ref, k_ref, vt_ref):
    for rs in _subtiles(hm_ref):
        z = _dot(hm_ref[rs, :], w_ref[...])
        cs = cs_ref[rs, :]
        cqn = _rms(z[:, :MQR], qn_ref[...]).astype(BF16)
        kvn = _rms(z[:, MQR:MQR + MKVR], kvn_ref[...]).astype(BF16)
        qz = _dot(cqn, wuq_ref[...]) * Q_SCALE
        kz = _dot(kvn, wuk_ref[...])
        vt = _dot_nt(wuvt_ref[...], kvn)
        for h in range(MH):
            base = h * MHD
            rot = _rope_fold(qz[:, base + MNOPE:base + MHD], cs)
            q_ref[h, rs, :] = jnp.concatenate([qz[:, base:base + MNOPE], rot], axis=1).astype(BF16)
        krot = _rope_fold(z[:, MQR + MKVR:], cs)
        lane = lax.broadcasted_iota(jnp.int32, krot.shape, 1)
        krot = jnp.where(lane < MROPE, krot, 0.0)
        for h in range(MH):
            k_ref[h, rs, :] = jnp.concatenate([kz[:, h * MNOPE:(h + 1) * MNOPE], krot], axis=1).astype(BF16)
            vt_ref[h, :, rs] = vt[h * MV:(h + 1) * MV, :].astype(BF16)


def _p4_call(hm, w, cs, qn, wuq, kvn, wuk, wuvt):
    tm = PROJ_TM
    return pl.pallas_call(
        _p4_kernel,
        grid=(NT // tm,),
        in_specs=[pl.BlockSpec((tm, D), lambda i: (i, 0)), _const((D, MQR + MKVR + 2 * MROPE)),
                  pl.BlockSpec((tm, 2 * MROPE), lambda i: (i, 0)),
                  _const((1, MQR)), _const((MQR, MH * MHD)),
                  _const((1, MKVR)), _const((MKVR, MH * MNOPE)), _const((MH * MV, MKVR))],
        out_specs=[pl.BlockSpec((MH, tm, MHD), lambda i: (0, i, 0)),
                   pl.BlockSpec((MH, tm, MHD), lambda i: (0, i, 0)),
                   pl.BlockSpec((MH, MV, tm), lambda i: (0, 0, i))],
        out_shape=[jax.ShapeDtypeStruct((MH, NT, MHD), BF16),
                   jax.ShapeDtypeStruct((MH, NT, MHD), BF16),
                   jax.ShapeDtypeStruct((MH, MV, NT), BF16)],
        compiler_params=_cparams(("arbitrary",)),
        name="proj_mla",
    )(hm, w, cs, qn, wuq, kvn, wuk, wuvt)


GLA_TILE = 256
GLA_NCH = GLA_TILE // GCHUNK


def _gla_local(qk_ref, v_ref, g_ref, backward):
    g = g_ref[...]
    ri = lax.broadcasted_iota(jnp.int32, (GLA_TILE, GLA_TILE), 0)
    ci = lax.broadcasted_iota(jnp.int32, (GLA_TILE, GLA_TILE), 1)
    same = (ri // GCHUNK) == (ci // GCHUNK)
    tri = jnp.where(same & ((ci >= ri) if backward else (ci <= ri)), 1.0, 0.0).astype(BF16)
    g_hi = g.astype(BF16)
    g_lo = (g - g_hi.astype(F32)).astype(BF16)
    bc = _dot(tri, g_hi) + _dot(tri, g_lo)
    r64 = lax.broadcasted_iota(jnp.int32, (GCHUNK, GCHUNK), 0)
    c64 = lax.broadcasted_iota(jnp.int32, (GCHUNK, GCHUNK), 1)
    mask = (c64 >= r64) if backward else (c64 <= r64)
    order = range(GLA_NCH - 1, -1, -1) if backward else range(GLA_NCH)
    items = []
    for c in order:
        r0 = c * GCHUNK
        for h in range(GH):
            b = bc[r0:r0 + GCHUNK, h * GDK:(h + 1) * GDK]
            bl = b[0:1] if backward else b[GCHUNK - 1:GCHUNK]
            q = qk_ref[r0:r0 + GCHUNK, h * GDK:(h + 1) * GDK].astype(F32)
            k = qk_ref[r0:r0 + GCHUNK, GKEY + h * GDK:GKEY + (h + 1) * GDK].astype(F32)
            qe = (q * jnp.exp(b)).astype(BF16)
            ke = (k * jnp.exp(-b)).astype(BF16)
            kd = (k * jnp.exp(bl - b)).astype(BF16)
            a = jnp.where(mask, _dot_nt(qe, ke), 0.0).astype(BF16)
            items.append(dict(r0=r0, h=h, qe=qe, kd=kd, a=a, dec=jnp.exp(bl)))
    for it in items:
        v = v_ref[it["r0"]:it["r0"] + GCHUNK, it["h"] * GDV:(it["h"] + 1) * GDV]
        it["oi"] = _dot(it.pop("a"), v)
        it["u"] = _dot_tn(v, it.pop("kd"))
    return items


def _gla_kernel(qkf_ref, vf_ref, gf_ref, qkb_ref, vb_ref, gb_ref, of_ref, ob_ref, s_ref):
    @pl.when(pl.program_id(1) == 0)
    def _():
        s_ref[...] = jnp.zeros_like(s_ref)

    scans = [(_gla_local(qkf_ref, vf_ref, gf_ref, False), of_ref, 0),
             (_gla_local(qkb_ref, vb_ref, gb_ref, True), ob_ref, 1)]
    state = [[s_ref[d, h] for h in range(GH)] for d in range(2)]
    for n in range(GLA_NCH * GH):
        for items, o_ref, d in scans:
            it = items[n]
            r0, h = it["r0"], it["h"]
            s = state[d][h]
            o_ref[r0:r0 + GCHUNK, h * GDV:(h + 1) * GDV] = it["oi"] + _dot_nt(it["qe"], s.astype(BF16))
            state[d][h] = s * it["dec"] + it["u"]
    for d in range(2):
        for h in range(GH):
            s_ref[d, h] = state[d][h]


def _gla_call(qk, v, lfb):
    nxt = L // GLA_TILE

    def fwd(b, t):
        return jnp.where(t == 0, NX // GLA_TILE + b, b * nxt + t - 1)

    def bwd(b, t):
        return jnp.where(t == 0, NX // GLA_TILE + b, b * nxt + nxt - t)

    def specs(blk, d):
        return [pl.BlockSpec((GLA_TILE, 2 * GKEY), lambda b, t: (blk(b, t), 0)),
                pl.BlockSpec((GLA_TILE, GVAL), lambda b, t: (blk(b, t), 0)),
                pl.BlockSpec((GLA_TILE, GKEY), lambda b, t: (blk(b, t), d))]

    return pl.pallas_call(
        _gla_kernel,
        grid=(B, 1 + nxt),
        in_specs=specs(fwd, 0) + specs(bwd, 1),
        out_specs=[pl.BlockSpec((GLA_TILE, GVAL), lambda b, t: (fwd(b, t), 0)),
                   pl.BlockSpec((GLA_TILE, GVAL), lambda b, t: (bwd(b, t), 0))],
        out_shape=[jax.ShapeDtypeStruct((NT, GVAL), F32), jax.ShapeDtypeStruct((NT, GVAL), F32)],
        scratch_shapes=[pltpu.VMEM((2, GH, GDV, GDK), F32)],
        compiler_params=_cparams(("arbitrary", "arbitrary")),
        name="gla",
    )(qk, v, lfb, qk, v, lfb)


ATT_TQ = 1024
ATT_KC = 512
ATT_AHEAD = 2


def _flash_t(q, chunks):
    m = l = acc = None
    ahead = [_dot_nt(chunks[j][0], q) for j in range(min(ATT_AHEAD, len(chunks)))]
    for j, (_, vt) in enumerate(chunks):
        s = ahead.pop(0)
        if j + ATT_AHEAD < len(chunks):
            ahead.append(_dot_nt(chunks[j + ATT_AHEAD][0], q))
        cm = jnp.max(s, axis=0, keepdims=True)
        if m is None:
            m = cm
            p = jnp.exp2(s - m)
            l = jnp.sum(p, axis=0, keepdims=True)
            acc = _dot(vt, p.astype(BF16))
        else:
            m_new = jnp.maximum(m, cm)
            alpha = jnp.exp2(m - m_new)
            p = jnp.exp2(s - m_new)
            l = alpha * l + jnp.sum(p, axis=0, keepdims=True)
            acc = alpha * acc + _dot(vt, p.astype(BF16))
            m = m_new
    return acc / l


def _attn_kernel(q_ref, kx_ref, kc_ref, vtx_ref, vtc_ref, o_ref):
    chunks = [(kc_ref[...], vtc_ref[...])]
    for j in range(L // ATT_KC):
        chunks.append((kx_ref[j * ATT_KC:(j + 1) * ATT_KC, :], vtx_ref[:, j * ATT_KC:(j + 1) * ATT_KC]))
    o_ref[...] = _flash_t(q_ref[...], chunks).T.astype(BF16)


def _attn_ctx_kernel(q_ref, kc_ref, vtc_ref, prev_ref, o_ref):
    del prev_ref
    o_ref[...] = _flash_t(q_ref[...], [(kc_ref[...], vtc_ref[...])]).T.astype(BF16)


def _attn_call(qa, ka, vta, with_ctx):
    nq = L // ATT_TQ
    out = pl.pallas_call(
        _attn_kernel,
        grid=(B, MH, nq),
        in_specs=[pl.BlockSpec((None, ATT_TQ, MHD), lambda b, h, i: (h, b * nq + i, 0)),
                  pl.BlockSpec((None, L, MHD), lambda b, h, i: (h, b, 0)),
                  pl.BlockSpec((None, LC, MHD), lambda b, h, i: (h, NX // LC + b, 0)),
                  pl.BlockSpec((None, MV, L), lambda b, h, i: (h, 0, b)),
                  pl.BlockSpec((None, MV, LC), lambda b, h, i: (h, 0, NX // LC + b))],
        out_specs=pl.BlockSpec((ATT_TQ, MV), lambda b, h, i: (b * nq + i, h)),
        out_shape=jax.ShapeDtypeStruct((NT if with_ctx else NX, MH * MV), BF16),
        compiler_params=_cparams(("arbitrary", "arbitrary", "arbitrary")),
        name="attn",
    )(qa, ka, ka, vta, vta)
    if not with_ctx:
        return out
    return pl.pallas_call(
        _attn_ctx_kernel,
        grid=(B, MH),
        in_specs=[pl.BlockSpec((None, LC, MHD), lambda b, h: (h, NX // LC + b, 0)),
                  pl.BlockSpec((None, LC, MHD), lambda b, h: (h, NX // LC + b, 0)),
                  pl.BlockSpec((None, MV, LC), lambda b, h: (h, 0, NX // LC + b)),
                  pl.BlockSpec(memory_space=pl.ANY)],
        out_specs=pl.BlockSpec((LC, MV), lambda b, h: (NX // LC + b, h)),
        out_shape=jax.ShapeDtypeStruct((NT, MH * MV), BF16),
        input_output_aliases={3: 0},
        compiler_params=_cparams(("arbitrary", "arbitrary")),
        name="attn_ctx",
    )(qa, ka, vta, out)


MERGE_TM = 512


def _merge_kernel(x_ref, of_ref, ob_ref, r_ref, gn_ref, zb_ref, u_ref, up_ref, un_ref, cw_ref,
                  c_ref, gt_ref, wa_ref, wb_ref, wc_ref, wo_ref, g5_ref, lw_ref, lb_ref, o_ref):
    tm = MERGE_TM
    c = _dot(c_ref[...], wc_ref[...])
    o = of_ref[...] + ob_ref[...]
    parts = []
    for h in range(GH):
        oh = o[:, h * GDV:(h + 1) * GDV]
        parts.append(oh * lax.rsqrt(jnp.mean(oh * oh, axis=-1, keepdims=True) + EPS))
    a_in = (jnp.concatenate(parts, axis=1) * gn_ref[...] * r_ref[...].astype(F32)).astype(BF16)
    u = u_ref[...].astype(F32)
    prev = up_ref[7:8, :].astype(F32)
    nxt = un_ref[0:1, :].astype(F32)
    loc = lax.broadcasted_iota(jnp.int32, (tm, 1), 0)
    row = loc + pl.program_id(0) * tm
    seq = jnp.where(row >= NX, LC, L)
    pos = row & (seq - 1)
    ul = jnp.where(loc == 0, prev, pltpu.roll(u, 1, 0))
    ul = jnp.where(pos == 0, 0.0, ul)
    ur = jnp.where(loc == tm - 1, nxt, pltpu.roll(u, tm - 1, 0))
    ur = jnp.where(pos == seq - 1, 0.0, ur)
    cv = ul * cw_ref[0:1, :] + u * cw_ref[1:2, :] + ur * cw_ref[2:3, :]
    b_in = (zb_ref[...].astype(F32) * cv).astype(BF16)
    for rs in _subtiles(x_ref):
        a = _dot(a_in[rs, :], wa_ref[...])
        bb = _dot(b_in[rs, :], wb_ref[...])
        mix = (gt_ref[rs, :D].astype(F32) * a + gt_ref[rs, D:2 * D].astype(F32) * bb
               + gt_ref[rs, 2 * D:].astype(F32) * c[rs, :])
        mx = _dot(mix.astype(BF16), wo_ref[...])
        z = ALPHA * x_ref[rs, :] + g5_ref[...] * mx
        o_ref[rs, :] = _layer_norm(z, lw_ref[...], lb_ref[...])


def _merge_call(x1, o_f, o_b, r, gnorm, zb, u, conv_w, c_pre, gates, wa, wb, wc, wo, g5, lw, lb, rows):
    tm = MERGE_TM
    row = lambda w: pl.BlockSpec((tm, w), lambda i: (i, 0))
    last8 = rows // 8 - 1
    return pl.pallas_call(
        _merge_kernel,
        grid=(rows // tm,),
        in_specs=[row(D), row(GVAL), row(GVAL), row(GVAL), _const((1, GVAL)),
                  row(CW), row(CW),
                  pl.BlockSpec((8, CW), lambda i: (jnp.maximum(i * (tm // 8) - 1, 0), 0)),
                  pl.BlockSpec((8, CW), lambda i: (jnp.minimum((i + 1) * (tm // 8), last8), 0)),
                  _const((3, CW)),
                  row(MH * MV), row(3 * D),
                  _const((GVAL, D)), _const((CW, D)), _const((MH * MV, D)), _const((D, D)),
                  _mod_spec(tm), _const((1, D)), _const((1, D))],
        out_specs=row(D),
        out_shape=jax.ShapeDtypeStruct((rows, D), F32),
        compiler_params=_cparams(("arbitrary",)),
        name="merge",
    )(x1, o_f, o_b, r, gnorm, zb, u, u, u, conv_w, c_pre, gates, wa, wb, wc, wo, g5, lw, lb)


def _rope_tables():
    t = jnp.arange(L, dtype=jnp.int32)
    quarter = MROPE // 4
    inv_freq = THETA ** (-jnp.arange(quarter, dtype=F32) / quarter)

    def cs(pos):
        ang = pos.astype(F32)[:, None] * inv_freq
        return jnp.cos(ang), jnp.sin(ang)

    cr, sr = cs(t // GRID_W)
    cc, sn = cs(t % GRID_W)
    tab = jnp.concatenate([cr, cr, cc, cc, -sr, sr, -sn, sn], axis=1)
    ident = jnp.concatenate([jnp.ones((NC, MROPE), F32), jnp.zeros((NC, MROPE), F32)], axis=1)
    return jnp.concatenate([jnp.tile(tab, (B, 1)), ident], axis=0)


def _swap_rope_cols(w):
    q = MROPE // 4
    return jnp.concatenate([w[..., q:2 * q], w[..., :q], w[..., 3 * q:], w[..., 2 * q:3 * q]], axis=-1)


def _layer_weights(l, w_in, gla_decay_w, gla_decay_b, mla_w_uq, mla_w_ukv):
    wi = w_in[l]
    o = 0
    cols = {}
    for name, n in (("q", GKEY), ("k", GKEY), ("v", GVAL), ("r", GVAL), ("gf", GRANK), ("gb", GRANK),
                    ("cb", CW), ("cc", CW), ("cx", CW), ("cq", MQR), ("ckv", MKVR), ("kr", MROPE),
                    ("zg", 3 * D)):
        cols[name] = wi[:, o:o + n]
        o += n
    pad = jnp.zeros((D, 128 - 2 * GRANK), F32)
    w1 = jnp.concatenate([cols["q"], cols["k"], cols["v"], cols["r"]], axis=1)
    wg = jnp.concatenate([cols["gf"], cols["gb"], pad], axis=1)
    w2 = jnp.concatenate([cols["cb"], cols["cc"], cols["cx"]], axis=1)
    w4 = jnp.concatenate([cols["cq"], cols["ckv"], cols["kr"], _swap_rope_cols(cols["kr"])], axis=1)
    dw = jnp.zeros((128, 2 * GKEY), F32)
    dw = dw.at[:GRANK, :GKEY].set(gla_decay_w[l, 0]).at[GRANK:2 * GRANK, GKEY:].set(gla_decay_w[l, 1])
    db = jnp.concatenate([gla_decay_b[l, 0], gla_decay_b[l, 1]])[None, :]
    wq = mla_w_uq[l].reshape(MQR, MH, MNOPE + MROPE)
    wq = jnp.concatenate([wq, _swap_rope_cols(wq[..., MNOPE:])], axis=-1).reshape(MQR, MH * MHD)
    wkv = mla_w_ukv[l].reshape(MKVR, MH, MNOPE + MV)
    wk = wkv[..., :MNOPE].reshape(MKVR, MH * MNOPE)
    wvt = wkv[..., MNOPE:].reshape(MKVR, MH * MV).T
    bf = lambda a: a.astype(BF16)
    return dict(w1=bf(w1), wg=bf(wg), w2=bf(w2), w3=bf(cols["zg"]), w4=bf(w4), dw=bf(dw), db=db, wq=bf(wq),
                wk=bf(wk), wvt=bf(wvt))


def kernel(x, c, ctx, c_ctx, ada_w, ada_b, ln_w, ln_b, ffn_w13, ffn_w2, w_in, b_gate, gla_decay_w,
           gla_decay_b, gla_norm, gla_proj, conv_w, conv_proj, mla_q_norm, mla_w_uq, mla_kv_norm,
           mla_w_ukv, mla_proj, w_out):
    assert x.shape == (B, L, D) and ctx.shape == (B, LC, D)
    c_all = jnp.concatenate([c, c_ctx[None, :], jnp.zeros((MODROWS - B - 1, D), F32)], axis=0)
    mods = _ada_call(c_all, ada_w, ada_b).reshape(DEPTH, MODROWS, NMOD, 1, D)
    cs = _rope_tables()
    h = jnp.concatenate([x.reshape(NX, D), ctx.reshape(NC, D)], axis=0)
    vec = lambda a: a.reshape(1, -1)
    for l in range(DEPTH):
        last = l == DEPTH - 1
        rows = NX if last else NT
        m = [mods[l, :, k] for k in range(NMOD)]
        lw = _layer_weights(l, w_in, gla_decay_w, gla_decay_b, mla_w_uq, mla_w_ukv)
        x1, hm = _ffn_call(h, NT, m[0:3], ffn_w13[l, 0].astype(BF16), ffn_w2[l, 0].astype(BF16),
                           vec(ln_w[l, 0]), vec(ln_b[l, 0]), mod_next=(m[3], m[4]))
        qk, v, r, lfb = _p1_call(hm, lw["w1"], lw["wg"], lw["dw"], lw["db"])
        zb, u = _p2_call(hm, lw["w2"], rows)
        gates = _p3_call(hm, lw["w3"], vec(b_gate[l]), rows)
        qa, ka, vta = _p4_call(hm, lw["w4"], cs, vec(mla_q_norm[l]), lw["wq"], vec(mla_kv_norm[l]),
                               lw["wk"], lw["wvt"])
        o_f, o_b = _gla_call(qk, v, lfb)
        c_pre = _attn_call(qa, ka, vta, with_ctx=not last)
        x2 = _merge_call(x1, o_f, o_b, r, vec(gla_norm[l]), zb, u, conv_w[l], c_pre, gates,
                         gla_proj[l].astype(BF16), conv_proj[l].astype(BF16), mla_proj[l].astype(BF16),
                         w_out[l].astype(BF16), m[5], vec(ln_w[l, 1]), vec(ln_b[l, 1]), rows)
        h = _ffn_call(x2, rows, m[6:9], ffn_w13[l, 1].astype(BF16), ffn_w2[l, 1].astype(BF16),
                      vec(ln_w[l, 2]), vec(ln_b[l, 2]))
    return h.reshape(B, L, D)
```

```python
import functools

import jax
import jax.numpy as jnp
from jax import lax
from jax.experimental import pallas as pl
from jax.experimental.pallas import tpu as pltpu

F32 = jnp.float32
BF16 = jnp.bfloat16

D = 1024
B = 8
L = 4096
DEPTH = 4
LC = 256
GRID_W = 64
DFF = 2816
GH, GDK, GDV = 4, 128, 256
GKEY, GVAL = GH * GDK, GH * GDV
GRANK = 16
GTEMP = 16.0
GCHUNK = 64
CW = 1024
MH, MQR, MKVR, MNOPE, MROPE, MV = 8, 512, 256, 128, 64, 128
THETA = 10000.0
NMOD = 9
EPS = 1e-6
ALPHA = (2.0 * DEPTH) ** 0.25

NX = B * L
NC = B * LC
NT = NX + NC
MODROWS = 16
MHD = 256
MVA = MV + 16

VMEM_LIMIT = 56 * 1024 * 1024


def _cparams(sem):
    return pltpu.CompilerParams(dimension_semantics=sem, vmem_limit_bytes=VMEM_LIMIT)


def _const(shape):
    n = len(shape)
    return pl.BlockSpec(shape, lambda *_: (0,) * n, pipeline_mode=pl.Buffered(1))


def _mod_spec(tm):
    per = L // tm
    return pl.BlockSpec((None, 1, D), lambda i: (jnp.minimum(i // per, B), 0, 0))


def _silu(x):
    return x * jax.nn.sigmoid(x)


def _layer_norm(z, w, b):
    mu = jnp.mean(z, axis=-1, keepdims=True)
    zc = z - mu
    var = jnp.mean(zc * zc, axis=-1, keepdims=True)
    return zc * lax.rsqrt(var + EPS) * w + b


def _rms(x, w):
    return x * lax.rsqrt(jnp.mean(x * x, axis=-1, keepdims=True) + EPS) * w


def _dot(a, b):
    return jnp.dot(a, b, preferred_element_type=F32)


def _dot_nt(a, b):
    return lax.dot_general(a, b, (((1,), (1,)), ((), ())), preferred_element_type=F32)


def _dot_tn(a, b):
    return lax.dot_general(a, b, (((0,), (0,)), ((), ())), preferred_element_type=F32)


def _ada_kernel(c_ref, w_ref, b_ref, o_ref):
    sc = _silu(c_ref[...]).astype(BF16)
    o_ref[...] = _dot(sc, w_ref[...].astype(BF16)) + b_ref[...]


def _ada_call(c_all, ada_w, ada_b):
    tn = 1152
    return pl.pallas_call(
        _ada_kernel,
        grid=(DEPTH, NMOD * D // tn),
        in_specs=[
            pl.BlockSpec((MODROWS, D), lambda l, j: (0, 0)),
            pl.BlockSpec((None, D, tn), lambda l, j: (l, 0, j)),
            pl.BlockSpec((None, 1, tn), lambda l, j: (l, 0, j)),
        ],
        out_specs=pl.BlockSpec((None, MODROWS, tn), lambda l, j: (l, 0, j)),
        out_shape=jax.ShapeDtypeStruct((DEPTH, MODROWS, NMOD * D), F32),
        compiler_params=_cparams(("arbitrary", "arbitrary")),
        name="ada",
    )(c_all, ada_w, ada_b.reshape(DEPTH, 1, NMOD * D))


SUB_ROWS = 256
PROJ_TM = 1024

def _ffn_kernel(x_ref, sh_ref, sc_ref, g_ref, w13_ref, w2_ref, lw_ref, lb_ref, *rest, emit_mod):
    for r0 in range(0, x_ref.shape[0], SUB_ROWS):
        rs = slice(r0, r0 + SUB_ROWS)
        x = x_ref[rs, :]
        hm = (x * (1.0 + sc_ref[...]) + sh_ref[...]).astype(BF16)
        a = _dot(hm, w13_ref[:, :DFF])
        g = _dot(hm, w13_ref[:, DFF:])
        act = (_silu(a) * g).astype(BF16)
        y = _dot(act, w2_ref[...])
        z = ALPHA * x + (0.5 * g_ref[...]) * y
        out = _layer_norm(z, lw_ref[...], lb_ref[...])
        if emit_mod:
            sh2_ref, sc2_ref, o_ref, om_ref = rest
            o_ref[rs, :] = out
            om_ref[rs, :] = (out * (1.0 + sc2_ref[...]) + sh2_ref[...]).astype(BF16)
        else:
            (o_ref,) = rest
            o_ref[rs, :] = out


def _ffn_call(h, rows, mod3, w13, w2, lw, lb, mod_next=None):
    tm = 1024
    emit = mod_next is not None
    row = pl.BlockSpec((tm, D), lambda i: (i, 0))
    in_specs = [row, _mod_spec(tm), _mod_spec(tm), _mod_spec(tm),
                _const((D, 2 * DFF)), _const((DFF, D)), _const((1, D)), _const((1, D))]
    args = [h, *mod3, w13, w2, lw, lb]
    out_shape = [jax.ShapeDtypeStruct((rows, D), F32)]
    out_specs = [row]
    if emit:
        in_specs += [_mod_spec(tm), _mod_spec(tm)]
        args += list(mod_next)
        out_shape.append(jax.ShapeDtypeStruct((rows, D), BF16))
        out_specs.append(row)
    res = pl.pallas_call(
        functools.partial(_ffn_kernel, emit_mod=emit),
        grid=(rows // tm,),
        in_specs=in_specs,
        out_specs=out_specs,
        out_shape=out_shape,
        compiler_params=_cparams(("arbitrary",)),
        name="ffn",
    )(*args)
    return res if emit else res[0]


def _subtiles(ref):
    return [slice(r0, r0 + SUB_ROWS) for r0 in range(0, ref.shape[0], SUB_ROWS)]


def _p1_kernel(hm_ref, w_ref, dw_ref, db_ref, qk_ref, v_ref, r_ref, lfb_ref):
    n0 = 128 + 2 * GKEY
    for rs in _subtiles(hm_ref):
        hm = hm_ref[rs, :]
        z0 = _dot(hm, w_ref[:, :n0])
        xg = _dot(z0[:, :128].astype(BF16), dw_ref[...]) + db_ref[...]
        lfb_ref[rs, :] = (jnp.minimum(xg, 0.0) - jnp.log(1.0 + jnp.exp(-jnp.abs(xg)))) * (1.0 / GTEMP)
        qk_ref[rs, :GKEY] = (z0[:, 128:128 + GKEY] * (GDK ** -0.5)).astype(BF16)
        qk_ref[rs, GKEY:] = z0[:, 128 + GKEY:].astype(BF16)
        z1 = _dot(hm, w_ref[:, n0:])
        v_ref[rs, :] = z1[:, :GVAL].astype(BF16)
        r_ref[rs, :] = _silu(z1[:, GVAL:]).astype(BF16)


def _p1_call(hm, w, dw, db):
    tm = PROJ_TM
    return pl.pallas_call(
        _p1_kernel,
        grid=(NT // tm,),
        in_specs=[pl.BlockSpec((tm, D), lambda i: (i, 0)),
                  _const((D, 128 + 2 * GKEY + 2 * GVAL)),
                  _const((128, 2 * GKEY)), _const((1, 2 * GKEY))],
        out_specs=[pl.BlockSpec((tm, 2 * GKEY), lambda i: (i, 0)),
                   pl.BlockSpec((tm, GVAL), lambda i: (i, 0)),
                   pl.BlockSpec((tm, GVAL), lambda i: (i, 0)),
                   pl.BlockSpec((tm, 2 * GKEY), lambda i: (i, 0))],
        out_shape=[jax.ShapeDtypeStruct((NT, 2 * GKEY), BF16),
                   jax.ShapeDtypeStruct((NT, GVAL), BF16),
                   jax.ShapeDtypeStruct((NT, GVAL), BF16),
                   jax.ShapeDtypeStruct((NT, 2 * GKEY), F32)],
        compiler_params=_cparams(("arbitrary",)),
        name="proj_gla",
    )(hm, w, dw, db)


def _p2_kernel(hm_ref, w_ref, zb_ref, u_ref):
    for rs in _subtiles(hm_ref):
        z = _dot(hm_ref[rs, :], w_ref[...])
        zb_ref[rs, :] = z[:, :CW].astype(BF16)
        u_ref[rs, :] = (z[:, CW:2 * CW] * z[:, 2 * CW:]).astype(BF16)


def _p2_call(hm, w, rows):
    tm = PROJ_TM
    return pl.pallas_call(
        _p2_kernel,
        grid=(rows // tm,),
        in_specs=[pl.BlockSpec((tm, D), lambda i: (i, 0)), _const((D, 3 * CW))],
        out_specs=[pl.BlockSpec((tm, CW), lambda i: (i, 0)),
                   pl.BlockSpec((tm, CW), lambda i: (i, 0))],
        out_shape=[jax.ShapeDtypeStruct((rows, CW), BF16),
                   jax.ShapeDtypeStruct((rows, CW), BF16)],
        compiler_params=_cparams(("arbitrary",)),
        name="proj_conv",
    )(hm, w)


def _p3_kernel(hm_ref, w_ref, b_ref, g_ref):
    for rs in _subtiles(hm_ref):
        z = _dot(hm_ref[rs, :], w_ref[...]) + b_ref[...]
        g_ref[rs, :] = jax.nn.sigmoid(z).astype(BF16)


def _p3_call(hm, w, b_gate, rows):
    tm = PROJ_TM
    return pl.pallas_call(
        _p3_kernel,
        grid=(rows // tm,),
        in_specs=[pl.BlockSpec((tm, D), lambda i: (i, 0)), _const((D, 3 * D)), _const((1, 3 * D))],
        out_specs=pl.BlockSpec((tm, 3 * D), lambda i: (i, 0)),
        out_shape=jax.ShapeDtypeStruct((rows, 3 * D), BF16),
        compiler_params=_cparams(("arbitrary",)),
        name="proj_gate",
    )(hm, w, b_gate)


def _rope_fold(t, cs):
    rr = t * cs
    return rr + pltpu.roll(rr, MROPE, 1)


Q_SCALE = (MNOPE + MROPE) ** -0.5 * 1.4426950408889634


def _p4_kernel(hm_ref, w_ref, cs_ref, qn_ref, wuq_ref, kvn_ref, wuk_ref, wuvt_ref, q_ref, k_ref, vt_ref):
    for rs in _subtiles(hm_ref):
        z = _dot(hm_ref[rs, :], w_ref[...])
        cs = cs_ref[rs, :]
        cqn = _rms(z[:, :MQR], qn_ref[...]).astype(BF16)
        kvn = _rms(z[:, MQR:MQR + MKVR], kvn_ref[...]).astype(BF16)
        qz = _dot(cqn, wuq_ref[...]) * Q_SCALE
        kz = _dot(kvn, wuk_ref[...])
        vt = _dot_nt(wuvt_ref[...], kvn)
        for h in range(MH):
            base = h * MHD
            rot = _rope_fold(qz[:, base + MNOPE:base + MHD], cs)
            q_ref[h, rs, :] = jnp.concatenate([qz[:, base:base + MNOPE], rot], axis=1).astype(BF16)
        krot = _rope_fold(z[:, MQR + MKVR:], cs)
        lane = lax.broadcasted_iota(jnp.int32, krot.shape, 1)
        krot = jnp.where(lane < MROPE, krot, 0.0)
        for h in range(MH):
            k_ref[h, rs, :] = jnp.concatenate([kz[:, h * MNOPE:(h + 1) * MNOPE], krot], axis=1).astype(BF16)
            vt_ref[h, :MV, rs] = vt[h * MV:(h + 1) * MV, :].astype(BF16)
            vt_ref[h, MV:, rs] = jnp.ones((MVA - MV, SUB_ROWS), BF16)


def _p4_call(hm, w, cs, qn, wuq, kvn, wuk, wuvt):
    tm = PROJ_TM
    return pl.pallas_call(
        _p4_kernel,
        grid=(NT // tm,),
        in_specs=[pl.BlockSpec((tm, D), lambda i: (i, 0)), _const((D, MQR + MKVR + 2 * MROPE)),
                  pl.BlockSpec((tm, 2 * MROPE), lambda i: (i, 0)),
                  _const((1, MQR)), _const((MQR, MH * MHD)),
                  _const((1, MKVR)), _const((MKVR, MH * MNOPE)), _const((MH * MV, MKVR))],
        out_specs=[pl.BlockSpec((MH, tm, MHD), lambda i: (0, i, 0)),
                   pl.BlockSpec((MH, tm, MHD), lambda i: (0, i, 0)),
                   pl.BlockSpec((MH, MVA, tm), lambda i: (0, 0, i))],
        out_shape=[jax.ShapeDtypeStruct((MH, NT, MHD), BF16),
                   jax.ShapeDtypeStruct((MH, NT, MHD), BF16),
                   jax.ShapeDtypeStruct((MH, MVA, NT), BF16)],
        compiler_params=_cparams(("arbitrary",)),
        name="proj_mla",
    )(hm, w, cs, qn, wuq, kvn, wuk, wuvt)


GLA_TILE = 256
GLA_NCH = GLA_TILE // GCHUNK


def _gla_local(qk_ref, v_ref, g_ref, backward):
    g = g_ref[...]
    ri = lax.broadcasted_iota(jnp.int32, (GLA_TILE, GLA_TILE), 0)
    ci = lax.broadcasted_iota(jnp.int32, (GLA_TILE, GLA_TILE), 1)
    same = (ri // GCHUNK) == (ci // GCHUNK)
    tri = jnp.where(same & ((ci >= ri) if backward else (ci <= ri)), 1.0, 0.0).astype(BF16)
    g_hi = g.astype(BF16)
    g_lo = (g - g_hi.astype(F32)).astype(BF16)
    bc = _dot(tri, g_hi) + _dot(tri, g_lo)
    r64 = lax.broadcasted_iota(jnp.int32, (GCHUNK, GCHUNK), 0)
    c64 = lax.broadcasted_iota(jnp.int32, (GCHUNK, GCHUNK), 1)
    mask = (c64 >= r64) if backward else (c64 <= r64)
    order = range(GLA_NCH - 1, -1, -1) if backward else range(GLA_NCH)
    items = []
    for c in order:
        r0 = c * GCHUNK
        for h in range(GH):
            b = bc[r0:r0 + GCHUNK, h * GDK:(h + 1) * GDK]
            bl = b[0:1] if backward else b[GCHUNK - 1:GCHUNK]
            q = qk_ref[r0:r0 + GCHUNK, h * GDK:(h + 1) * GDK].astype(F32)
            k = qk_ref[r0:r0 + GCHUNK, GKEY + h * GDK:GKEY + (h + 1) * GDK].astype(F32)
            qe = (q * jnp.exp(b)).astype(BF16)
            ke = (k * jnp.exp(-b)).astype(BF16)
            kd = (k * jnp.exp(bl - b)).astype(BF16)
            a = jnp.where(mask, _dot_nt(qe, ke), 0.0).astype(BF16)
            items.append(dict(r0=r0, h=h, qe=qe, kd=kd, a=a, dec=jnp.exp(bl)))
    for it in items:
        v = v_ref[it["r0"]:it["r0"] + GCHUNK, it["h"] * GDV:(it["h"] + 1) * GDV]
        it["oi"] = _dot(it.pop("a"), v)
        it["u"] = _dot_tn(v, it.pop("kd"))
    return items


def _gla_kernel(qkf_ref, vf_ref, gf_ref, qkb_ref, vb_ref, gb_ref, of_ref, ob_ref, s_ref):
    @pl.when(pl.program_id(1) == 0)
    def _():
        s_ref[...] = jnp.zeros_like(s_ref)

    scans = [(_gla_local(qkf_ref, vf_ref, gf_ref, False), of_ref, 0),
             (_gla_local(qkb_ref, vb_ref, gb_ref, True), ob_ref, 1)]
    state = [[s_ref[d, h] for h in range(GH)] for d in range(2)]
    for n in range(GLA_NCH * GH):
        for items, o_ref, d in scans:
            it = items[n]
            r0, h = it["r0"], it["h"]
            s = state[d][h]
            o_ref[r0:r0 + GCHUNK, h * GDV:(h + 1) * GDV] = it["oi"] + _dot_nt(it["qe"], s.astype(BF16))
            state[d][h] = s * it["dec"] + it["u"]
    for d in range(2):
        for h in range(GH):
            s_ref[d, h] = state[d][h]


def _gla_call(qk, v, lfb):
    nxt = L // GLA_TILE

    def fwd(b, t):
        return jnp.where(t == 0, NX // GLA_TILE + b, b * nxt + t - 1)

    def bwd(b, t):
        return jnp.where(t == 0, NX // GLA_TILE + b, b * nxt + nxt - t)

    def specs(blk, d):
        return [pl.BlockSpec((GLA_TILE, 2 * GKEY), lambda b, t: (blk(b, t), 0)),
                pl.BlockSpec((GLA_TILE, GVAL), lambda b, t: (blk(b, t), 0)),
                pl.BlockSpec((GLA_TILE, GKEY), lambda b, t: (blk(b, t), d))]

    return pl.pallas_call(
        _gla_kernel,
        grid=(B, 1 + nxt),
        in_specs=specs(fwd, 0) + specs(bwd, 1),
        out_specs=[pl.BlockSpec((GLA_TILE, GVAL), lambda b, t: (fwd(b, t), 0)),
                   pl.BlockSpec((GLA_TILE, GVAL), lambda b, t: (bwd(b, t), 0))],
        out_shape=[jax.ShapeDtypeStruct((NT, GVAL), F32), jax.ShapeDtypeStruct((NT, GVAL), F32)],
        scratch_shapes=[pltpu.VMEM((2, GH, GDV, GDK), F32)],
        compiler_params=_cparams(("arbitrary", "arbitrary")),
        name="gla",
    )(qk, v, lfb, qk, v, lfb)


ATT_TQ = 1024
ATT_KC = 512
ATT_AHEAD = 2


def _flash_t(q, chunks):
    m = acc = None
    ahead = [_dot_nt(chunks[j][0], q) for j in range(min(ATT_AHEAD, len(chunks)))]
    for j, (_, vt) in enumerate(chunks):
        s = ahead.pop(0)
        if j + ATT_AHEAD < len(chunks):
            ahead.append(_dot_nt(chunks[j + ATT_AHEAD][0], q))
        cm = jnp.max(s, axis=0, keepdims=True)
        if m is None:
            m = cm
            acc = _dot(vt, jnp.exp2(s - m).astype(BF16))
        else:
            m_new = jnp.maximum(m, cm)
            acc = jnp.exp2(m - m_new) * acc + _dot(vt, jnp.exp2(s - m_new).astype(BF16))
            m = m_new
    return acc[:MV, :] / acc[MV:MV + 1, :]


def _attn_kernel(q_ref, kx_ref, kc_ref, vtx_ref, vtc_ref, o_ref):
    chunks = [(kc_ref[...], vtc_ref[...])]
    for j in range(L // ATT_KC):
        chunks.append((kx_ref[j * ATT_KC:(j + 1) * ATT_KC, :], vtx_ref[:, j * ATT_KC:(j + 1) * ATT_KC]))
    o_ref[...] = _flash_t(q_ref[...], chunks).T.astype(BF16)


def _attn_ctx_kernel(q_ref, kc_ref, vtc_ref, prev_ref, o_ref):
    del prev_ref
    o_ref[...] = _flash_t(q_ref[...], [(kc_ref[...], vtc_ref[...])]).T.astype(BF16)


def _attn_call(qa, ka, vta, with_ctx):
    nq = L // ATT_TQ
    out = pl.pallas_call(
        _attn_kernel,
        grid=(B, MH, nq),
        in_specs=[pl.BlockSpec((None, ATT_TQ, MHD), lambda b, h, i: (h, b * nq + i, 0)),
                  pl.BlockSpec((None, L, MHD), lambda b, h, i: (h, b, 0)),
                  pl.BlockSpec((None, LC, MHD), lambda b, h, i: (h, NX // LC + b, 0)),
                  pl.BlockSpec((None, MVA, L), lambda b, h, i: (h, 0, b)),
                  pl.BlockSpec((None, MVA, LC), lambda b, h, i: (h, 0, NX // LC + b))],
        out_specs=pl.BlockSpec((ATT_TQ, MV), lambda b, h, i: (b * nq + i, h)),
        out_shape=jax.ShapeDtypeStruct((NT if with_ctx else NX, MH * MV), BF16),
        compiler_params=_cparams(("arbitrary", "arbitrary", "arbitrary")),
        name="attn",
    )(qa, ka, ka, vta, vta)
    if not with_ctx:
        return out
    return pl.pallas_call(
        _attn_ctx_kernel,
        grid=(B, MH),
        in_specs=[pl.BlockSpec((None, LC, MHD), lambda b, h: (h, NX // LC + b, 0)),
                  pl.BlockSpec((None, LC, MHD), lambda b, h: (h, NX // LC + b, 0)),
                  pl.BlockSpec((None, MVA, LC), lambda b, h: (h, 0, NX // LC + b)),
                  pl.BlockSpec(memory_space=pl.ANY)],
        out_specs=pl.BlockSpec((LC, MV), lambda b, h: (NX // LC + b, h)),
        out_shape=jax.ShapeDtypeStruct((NT, MH * MV), BF16),
        input_output_aliases={3: 0},
        compiler_params=_cparams(("arbitrary", "arbitrary")),
        name="attn_ctx",
    )(qa, ka, vta, out)


MERGE_TM = 512


def _merge_kernel(x_ref, of_ref, ob_ref, r_ref, gn_ref, zb_ref, u_ref, up_ref, un_ref, cw_ref,
                  c_ref, gt_ref, wa_ref, wb_ref, wc_ref, wo_ref, g5_ref, lw_ref, lb_ref, o_ref):
    tm = MERGE_TM
    c = _dot(c_ref[...], wc_ref[...])
    o = of_ref[...] + ob_ref[...]
    parts = []
    for h in range(GH):
        oh = o[:, h * GDV:(h + 1) * GDV]
        parts.append(oh * lax.rsqrt(jnp.mean(oh * oh, axis=-1, keepdims=True) + EPS))
    a_in = (jnp.concatenate(parts, axis=1) * gn_ref[...] * r_ref[...].astype(F32)).astype(BF16)
    u = u_ref[...].astype(F32)
    prev = up_ref[7:8, :].astype(F32)
    nxt = un_ref[0:1, :].astype(F32)
    loc = lax.broadcasted_iota(jnp.int32, (tm, 1), 0)
    row = loc + pl.program_id(0) * tm
    seq = jnp.where(row >= NX, LC, L)
    pos = row & (seq - 1)
    ul = jnp.where(loc == 0, prev, pltpu.roll(u, 1, 0))
    ul = jnp.where(pos == 0, 0.0, ul)
    ur = jnp.where(loc == tm - 1, nxt, pltpu.roll(u, tm - 1, 0))
    ur = jnp.where(pos == seq - 1, 0.0, ur)
    cv = ul * cw_ref[0:1, :] + u * cw_ref[1:2, :] + ur * cw_ref[2:3, :]
    b_in = (zb_ref[...].astype(F32) * cv).astype(BF16)
    for rs in _subtiles(x_ref):
        a = _dot(a_in[rs, :], wa_ref[...])
        bb = _dot(b_in[rs, :], wb_ref[...])
        mix = (gt_ref[rs, :D].astype(F32) * a + gt_ref[rs, D:2 * D].astype(F32) * bb
               + gt_ref[rs, 2 * D:].astype(F32) * c[rs, :])
        mx = _dot(mix.astype(BF16), wo_ref[...])
        z = ALPHA * x_ref[rs, :] + g5_ref[...] * mx
        o_ref[rs, :] = _layer_norm(z, lw_ref[...], lb_ref[...])


def _merge_call(x1, o_f, o_b, r, gnorm, zb, u, conv_w, c_pre, gates, wa, wb, wc, wo, g5, lw, lb, rows):
    tm = MERGE_TM
    row = lambda w: pl.BlockSpec((tm, w), lambda i: (i, 0))
    last8 = rows // 8 - 1
    return pl.pallas_call(
        _merge_kernel,
        grid=(rows // tm,),
        in_specs=[row(D), row(GVAL), row(GVAL), row(GVAL), _const((1, GVAL)),
                  row(CW), row(CW),
                  pl.BlockSpec((8, CW), lambda i: (jnp.maximum(i * (tm // 8) - 1, 0), 0)),
                  pl.BlockSpec((8, CW), lambda i: (jnp.minimum((i + 1) * (tm // 8), last8), 0)),
                  _const((3, CW)),
                  row(MH * MV), row(3 * D),
                  _const((GVAL, D)), _const((CW, D)), _const((MH * MV, D)), _const((D, D)),
                  _mod_spec(tm), _const((1, D)), _const((1, D))],
        out_specs=row(D),
        out_shape=jax.ShapeDtypeStruct((rows, D), F32),
        compiler_params=_cparams(("arbitrary",)),
        name="merge",
    )(x1, o_f, o_b, r, gnorm, zb, u, u, u, conv_w, c_pre, gates, wa, wb, wc, wo, g5, lw, lb)


def _rope_tables():
    t = jnp.arange(L, dtype=jnp.int32)
    quarter = MROPE // 4
    inv_freq = THETA ** (-jnp.arange(quarter, dtype=F32) / quarter)

    def cs(pos):
        ang = pos.astype(F32)[:, None] * inv_freq
        return jnp.cos(ang), jnp.sin(ang)

    cr, sr = cs(t // GRID_W)
    cc, sn = cs(t % GRID_W)
    tab = jnp.concatenate([cr, cr, cc, cc, -sr, sr, -sn, sn], axis=1)
    ident = jnp.concatenate([jnp.ones((NC, MROPE), F32), jnp.zeros((NC, MROPE), F32)], axis=1)
    return jnp.concatenate([jnp.tile(tab, (B, 1)), ident], axis=0)


def _swap_rope_cols(w):
    q = MROPE // 4
    return jnp.concatenate([w[..., q:2 * q], w[..., :q], w[..., 3 * q:], w[..., 2 * q:3 * q]], axis=-1)


def _layer_weights(l, w_in, gla_decay_w, gla_decay_b, mla_w_uq, mla_w_ukv):
    wi = w_in[l]
    o = 0
    cols = {}
    for name, n in (("q", GKEY), ("k", GKEY), ("v", GVAL), ("r", GVAL), ("gf", GRANK), ("gb", GRANK),
                    ("cb", CW), ("cc", CW), ("cx", CW), ("cq", MQR), ("ckv", MKVR), ("kr", MROPE),
                    ("zg", 3 * D)):
        cols[name] = wi[:, o:o + n]
        o += n
    pad = jnp.zeros((D, 128 - 2 * GRANK), F32)
    w1 = jnp.concatenate([cols["gf"], cols["gb"], pad, cols["q"], cols["k"], cols["v"], cols["r"]], axis=1)
    w2 = jnp.concatenate([cols["cb"], cols["cc"], cols["cx"]], axis=1)
    w4 = jnp.concatenate([cols["cq"], cols["ckv"], cols["kr"], _swap_rope_cols(cols["kr"])], axis=1)
    dw = jnp.zeros((128, 2 * GKEY), F32)
    dw = dw.at[:GRANK, :GKEY].set(gla_decay_w[l, 0]).at[GRANK:2 * GRANK, GKEY:].set(gla_decay_w[l, 1])
    db = jnp.concatenate([gla_decay_b[l, 0], gla_decay_b[l, 1]])[None, :]
    wq = mla_w_uq[l].reshape(MQR, MH, MNOPE + MROPE)
    wq = jnp.concatenate([wq, _swap_rope_cols(wq[..., MNOPE:])], axis=-1).reshape(MQR, MH * MHD)
    wkv = mla_w_ukv[l].reshape(MKVR, MH, MNOPE + MV)
    wk = wkv[..., :MNOPE].reshape(MKVR, MH * MNOPE)
    wvt = wkv[..., MNOPE:].reshape(MKVR, MH * MV).T
    bf = lambda a: a.astype(BF16)
    return dict(w1=bf(w1), w2=bf(w2), w3=bf(cols["zg"]), w4=bf(w4), dw=bf(dw), db=db, wq=bf(wq),
                wk=bf(wk), wvt=bf(wvt))


def kernel(x, c, ctx, c_ctx, ada_w, ada_b, ln_w, ln_b, ffn_w13, ffn_w2, w_in, b_gate, gla_decay_w,
           gla_decay_b, gla_norm, gla_proj, conv_w, conv_proj, mla_q_norm, mla_w_uq, mla_kv_norm,
           mla_w_ukv, mla_proj, w_out):
    assert x.shape == (B, L, D) and ctx.shape == (B, LC, D)
    c_all = jnp.concatenate([c, c_ctx[None, :], jnp.zeros((MODROWS - B - 1, D), F32)], axis=0)
    mods = _ada_call(c_all, ada_w, ada_b).reshape(DEPTH, MODROWS, NMOD, 1, D)
    cs = _rope_tables()
    h = jnp.concatenate([x.reshape(NX, D), ctx.reshape(NC, D)], axis=0)
    vec = lambda a: a.reshape(1, -1)
    for l in range(DEPTH):
        last = l == DEPTH - 1
        rows = NX if last else NT
        m = [mods[l, :, k] for k in range(NMOD)]
        lw = _layer_weights(l, w_in, gla_decay_w, gla_decay_b, mla_w_uq, mla_w_ukv)
        x1, hm = _ffn_call(h, NT, m[0:3], ffn_w13[l, 0].astype(BF16), ffn_w2[l, 0].astype(BF16),
                           vec(ln_w[l, 0]), vec(ln_b[l, 0]), mod_next=(m[3], m[4]))
        qk, v, r, lfb = _p1_call(hm, lw["w1"], lw["dw"], lw["db"])
        zb, u = _p2_call(hm, lw["w2"], rows)
        gates = _p3_call(hm, lw["w3"], vec(b_gate[l]), rows)
        qa, ka, vta = _p4_call(hm, lw["w4"], cs, vec(mla_q_norm[l]), lw["wq"], vec(mla_kv_norm[l]),
                               lw["wk"], lw["wvt"])
        o_f, o_b = _gla_call(qk, v, lfb)
        c_pre = _attn_call(qa, ka, vta, with_ctx=not last)
        x2 = _merge_call(x1, o_f, o_b, r, vec(gla_norm[l]), zb, u, conv_w[l], c_pre, gates,
                         gla_proj[l].astype(BF16), conv_proj[l].astype(BF16), mla_proj[l].astype(BF16),
                         w_out[l].astype(BF16), m[5], vec(ln_w[l, 1]), vec(ln_b[l, 1]), rows)
        h = _ffn_call(x2, rows, m[6:9], ffn_w13[l, 1].astype(BF16), ffn_w2[l, 1].astype(BF16),
                      vec(ln_w[l, 2]), vec(ln_b[l, 2]))
    return h.reshape(B, L, D)
```

```python
import functools

import jax
import jax.numpy as jnp
from jax import lax
from jax.experimental import pallas as pl
from jax.experimental.pallas import tpu as pltpu

F32 = jnp.float32
BF16 = jnp.bfloat16

D = 1024
B = 8
L = 4096
DEPTH = 4
LC = 256
GRID_W = 64
DFF = 2816
GH, GDK, GDV = 4, 128, 256
GKEY, GVAL = GH * GDK, GH * GDV
GRANK = 16
GTEMP = 16.0
GCHUNK = 64
CW = 1024
MH, MQR, MKVR, MNOPE, MROPE, MV = 8, 512, 256, 128, 64, 128
THETA = 10000.0
NMOD = 9
EPS = 1e-6
ALPHA = (2.0 * DEPTH) ** 0.25

NX = B * L
NC = B * LC
NT = NX + NC
MODROWS = 16
MHD = 256
MVA = MV + 16

VMEM_LIMIT = 56 * 1024 * 1024


def _cparams(sem):
    return pltpu.CompilerParams(dimension_semantics=sem, vmem_limit_bytes=VMEM_LIMIT)


def _const(shape):
    n = len(shape)
    return pl.BlockSpec(shape, lambda *_: (0,) * n, pipeline_mode=pl.Buffered(1))


def _mod_spec(tm):
    per = L // tm
    return pl.BlockSpec((None, 1, D), lambda i: (jnp.minimum(i // per, B), 0, 0))


def _silu(x):
    return x * jax.nn.sigmoid(x)


def _layer_norm(z, w, b):
    mu = jnp.mean(z, axis=-1, keepdims=True)
    zc = z - mu
    var = jnp.mean(zc * zc, axis=-1, keepdims=True)
    return zc * lax.rsqrt(var + EPS) * w + b


def _rms(x, w):
    return x * lax.rsqrt(jnp.mean(x * x, axis=-1, keepdims=True) + EPS) * w


def _dot(a, b):
    return jnp.dot(a, b, preferred_element_type=F32)


def _dot_nt(a, b):
    return lax.dot_general(a, b, (((1,), (1,)), ((), ())), preferred_element_type=F32)


def _dot_tn(a, b):
    return lax.dot_general(a, b, (((0,), (0,)), ((), ())), preferred_element_type=F32)


def _ada_kernel(c_ref, w_ref, b_ref, o_ref):
    sc = _silu(c_ref[...]).astype(BF16)
    o_ref[...] = _dot(sc, w_ref[...].astype(BF16)) + b_ref[...]


def _ada_call(c_all, ada_w, ada_b):
    tn = 1152
    return pl.pallas_call(
        _ada_kernel,
        grid=(DEPTH, NMOD * D // tn),
        in_specs=[
            pl.BlockSpec((MODROWS, D), lambda l, j: (0, 0)),
            pl.BlockSpec((None, D, tn), lambda l, j: (l, 0, j)),
            pl.BlockSpec((None, 1, tn), lambda l, j: (l, 0, j)),
        ],
        out_specs=pl.BlockSpec((None, MODROWS, tn), lambda l, j: (l, 0, j)),
        out_shape=jax.ShapeDtypeStruct((DEPTH, MODROWS, NMOD * D), F32),
        compiler_params=_cparams(("arbitrary", "arbitrary")),
        name="ada",
    )(c_all, ada_w, ada_b.reshape(DEPTH, 1, NMOD * D))


SUB_ROWS = 256
PROJ_TM = 1024

def _ffn_kernel(x_ref, sh_ref, sc_ref, g_ref, w13_ref, w2_ref, lw_ref, lb_ref, *rest, emit_mod):
    for r0 in range(0, x_ref.shape[0], SUB_ROWS):
        rs = slice(r0, r0 + SUB_ROWS)
        x = x_ref[rs, :]
        hm = (x * (1.0 + sc_ref[...]) + sh_ref[...]).astype(BF16)
        a = _dot(hm, w13_ref[:, :DFF])
        g = _dot(hm, w13_ref[:, DFF:])
        act = (_silu(a) * g).astype(BF16)
        y = _dot(act, w2_ref[...])
        z = ALPHA * x + (0.5 * g_ref[...]) * y
        out = _layer_norm(z, lw_ref[...], lb_ref[...])
        if emit_mod:
            sh2_ref, sc2_ref, o_ref, om_ref = rest
            o_ref[rs, :] = out
            om_ref[rs, :] = (out * (1.0 + sc2_ref[...]) + sh2_ref[...]).astype(BF16)
        else:
            (o_ref,) = rest
            o_ref[rs, :] = out


def _ffn_call(h, rows, mod3, w13, w2, lw, lb, mod_next=None):
    tm = 1024
    emit = mod_next is not None
    row = pl.BlockSpec((tm, D), lambda i: (i, 0))
    in_specs = [row, _mod_spec(tm), _mod_spec(tm), _mod_spec(tm),
                _const((D, 2 * DFF)), _const((DFF, D)), _const((1, D)), _const((1, D))]
    args = [h, *mod3, w13, w2, lw, lb]
    out_shape = [jax.ShapeDtypeStruct((rows, D), F32)]
    out_specs = [row]
    if emit:
        in_specs += [_mod_spec(tm), _mod_spec(tm)]
        args += list(mod_next)
        out_shape.append(jax.ShapeDtypeStruct((rows, D), BF16))
        out_specs.append(row)
    res = pl.pallas_call(
        functools.partial(_ffn_kernel, emit_mod=emit),
        grid=(rows // tm,),
        in_specs=in_specs,
        out_specs=out_specs,
        out_shape=out_shape,
        compiler_params=_cparams(("arbitrary",)),
        name="ffn",
    )(*args)
    return res if emit else res[0]


def _subtiles(ref):
    return [slice(r0, r0 + SUB_ROWS) for r0 in range(0, ref.shape[0], SUB_ROWS)]


def _p1_kernel(hm_ref, w_ref, dw_ref, db_ref, qk_ref, v_ref, r_ref, lfb_ref):
    n0 = 128 + 2 * GKEY
    for rs in _subtiles(hm_ref):
        hm = hm_ref[rs, :]
        z0 = _dot(hm, w_ref[:, :n0])
        xg = _dot(z0[:, :128].astype(BF16), dw_ref[...]) + db_ref[...]
        lfb_ref[rs, :] = (jnp.minimum(xg, 0.0) - jnp.log(1.0 + jnp.exp(-jnp.abs(xg)))) * (1.0 / GTEMP)
        qk_ref[rs, :GKEY] = (z0[:, 128:128 + GKEY] * (GDK ** -0.5)).astype(BF16)
        qk_ref[rs, GKEY:] = z0[:, 128 + GKEY:].astype(BF16)
        z1 = _dot(hm, w_ref[:, n0:])
        v_ref[rs, :] = z1[:, :GVAL].astype(BF16)
        r_ref[rs, :] = _silu(z1[:, GVAL:]).astype(BF16)


def _p1_call(hm, w, dw, db):
    tm = PROJ_TM
    return pl.pallas_call(
        _p1_kernel,
        grid=(NT // tm,),
        in_specs=[pl.BlockSpec((tm, D), lambda i: (i, 0)),
                  _const((D, 128 + 2 * GKEY + 2 * GVAL)),
                  _const((128, 2 * GKEY)), _const((1, 2 * GKEY))],
        out_specs=[pl.BlockSpec((tm, 2 * GKEY), lambda i: (i, 0)),
                   pl.BlockSpec((tm, GVAL), lambda i: (i, 0)),
                   pl.BlockSpec((tm, GVAL), lambda i: (i, 0)),
                   pl.BlockSpec((tm, 2 * GKEY), lambda i: (i, 0))],
        out_shape=[jax.ShapeDtypeStruct((NT, 2 * GKEY), BF16),
                   jax.ShapeDtypeStruct((NT, GVAL), BF16),
                   jax.ShapeDtypeStruct((NT, GVAL), BF16),
                   jax.ShapeDtypeStruct((NT, 2 * GKEY), F32)],
        compiler_params=_cparams(("arbitrary",)),
        name="proj_gla",
    )(hm, w, dw, db)


def _p2_kernel(hm_ref, w_ref, zb_ref, u_ref):
    for rs in _subtiles(hm_ref):
        z = _dot(hm_ref[rs, :], w_ref[...])
        zb_ref[rs, :] = z[:, :CW].astype(BF16)
        u_ref[rs, :] = (z[:, CW:2 * CW] * z[:, 2 * CW:]).astype(BF16)


def _p2_call(hm, w, rows):
    tm = PROJ_TM
    return pl.pallas_call(
        _p2_kernel,
        grid=(rows // tm,),
        in_specs=[pl.BlockSpec((tm, D), lambda i: (i, 0)), _const((D, 3 * CW))],
        out_specs=[pl.BlockSpec((tm, CW), lambda i: (i, 0)),
                   pl.BlockSpec((tm, CW), lambda i: (i, 0))],
        out_shape=[jax.ShapeDtypeStruct((rows, CW), BF16),
                   jax.ShapeDtypeStruct((rows, CW), BF16)],
        compiler_params=_cparams(("arbitrary",)),
        name="proj_conv",
    )(hm, w)


def _p3_kernel(hm_ref, w_ref, b_ref, g_ref):
    for rs in _subtiles(hm_ref):
        z = _dot(hm_ref[rs, :], w_ref[...]) + b_ref[...]
        g_ref[rs, :] = jax.nn.sigmoid(z).astype(BF16)


def _p3_call(hm, w, b_gate, rows):
    tm = PROJ_TM
    return pl.pallas_call(
        _p3_kernel,
        grid=(rows // tm,),
        in_specs=[pl.BlockSpec((tm, D), lambda i: (i, 0)), _const((D, 3 * D)), _const((1, 3 * D))],
        out_specs=pl.BlockSpec((tm, 3 * D), lambda i: (i, 0)),
        out_shape=jax.ShapeDtypeStruct((rows, 3 * D), BF16),
        compiler_params=_cparams(("arbitrary",)),
        name="proj_gate",
    )(hm, w, b_gate)


def _rope_fold(t, cs):
    rr = t * cs
    return rr + pltpu.roll(rr, MROPE, 1)


Q_SCALE = (MNOPE + MROPE) ** -0.5 * 1.4426950408889634


def _p4_kernel(hm_ref, w_ref, cs_ref, qn_ref, wuq_ref, kvn_ref, wuk_ref, wuvt_ref, q_ref, k_ref, vt_ref):
    for rs in _subtiles(hm_ref):
        z = _dot(hm_ref[rs, :], w_ref[...])
        cs = cs_ref[rs, :]
        cqn = _rms(z[:, :MQR], qn_ref[...]).astype(BF16)
        kvn = _rms(z[:, MQR:MQR + MKVR], kvn_ref[...]).astype(BF16)
        qz = _dot(cqn, wuq_ref[...]) * Q_SCALE
        kz = _dot(kvn, wuk_ref[...])
        vt = _dot_nt(wuvt_ref[...], kvn)
        for h in range(MH):
            base = h * MHD
            rot = _rope_fold(qz[:, base + MNOPE:base + MHD], cs)
            q_ref[h, rs, :] = jnp.concatenate([qz[:, base:base + MNOPE], rot], axis=1).astype(BF16)
        krot = _rope_fold(z[:, MQR + MKVR:], cs)
        lane = lax.broadcasted_iota(jnp.int32, krot.shape, 1)
        krot = jnp.where(lane < MROPE, krot, 0.0)
        for h in range(MH):
            k_ref[h, rs, :] = jnp.concatenate([kz[:, h * MNOPE:(h + 1) * MNOPE], krot], axis=1).astype(BF16)
            vt_ref[h, :MV, rs] = vt[h * MV:(h + 1) * MV, :].astype(BF16)
            vt_ref[h, MV:, rs] = jnp.ones((MVA - MV, SUB_ROWS), BF16)


def _p4_call(hm, w, cs, qn, wuq, kvn, wuk, wuvt):
    tm = PROJ_TM
    return pl.pallas_call(
        _p4_kernel,
        grid=(NT // tm,),
        in_specs=[pl.BlockSpec((tm, D), lambda i: (i, 0)), _const((D, MQR + MKVR + 2 * MROPE)),
                  pl.BlockSpec((tm, 2 * MROPE), lambda i: (i, 0)),
                  _const((1, MQR)), _const((MQR, MH * MHD)),
                  _const((1, MKVR)), _const((MKVR, MH * MNOPE)), _const((MH * MV, MKVR))],
        out_specs=[pl.BlockSpec((MH, tm, MHD), lambda i: (0, i, 0)),
                   pl.BlockSpec((MH, tm, MHD), lambda i: (0, i, 0)),
                   pl.BlockSpec((MH, MVA, tm), lambda i: (0, 0, i))],
        out_shape=[jax.ShapeDtypeStruct((MH, NT, MHD), BF16),
                   jax.ShapeDtypeStruct((MH, NT, MHD), BF16),
                   jax.ShapeDtypeStruct((MH, MVA, NT), BF16)],
        compiler_params=_cparams(("arbitrary",)),
        name="proj_mla",
    )(hm, w, cs, qn, wuq, kvn, wuk, wuvt)


GLA_TILE = 256
GLA_NCH = GLA_TILE // GCHUNK


def _gla_local(qk_ref, v_ref, g_ref, backward):
    g = g_ref[...]
    ri = lax.broadcasted_iota(jnp.int32, (GLA_TILE, GLA_TILE), 0)
    ci = lax.broadcasted_iota(jnp.int32, (GLA_TILE, GLA_TILE), 1)
    same = (ri // GCHUNK) == (ci // GCHUNK)
    tri = jnp.where(same & ((ci >= ri) if backward else (ci <= ri)), 1.0, 0.0).astype(BF16)
    g_hi = g.astype(BF16)
    g_lo = (g - g_hi.astype(F32)).astype(BF16)
    bc = _dot(tri, g_hi) + _dot(tri, g_lo)
    r64 = lax.broadcasted_iota(jnp.int32, (GCHUNK, GCHUNK), 0)
    c64 = lax.broadcasted_iota(jnp.int32, (GCHUNK, GCHUNK), 1)
    mask = (c64 >= r64) if backward else (c64 <= r64)
    order = range(GLA_NCH - 1, -1, -1) if backward else range(GLA_NCH)
    items = []
    for c in order:
        r0 = c * GCHUNK
        for h in range(GH):
            b = bc[r0:r0 + GCHUNK, h * GDK:(h + 1) * GDK]
            bl = b[0:1] if backward else b[GCHUNK - 1:GCHUNK]
            q = qk_ref[r0:r0 + GCHUNK, h * GDK:(h + 1) * GDK].astype(F32)
            k = qk_ref[r0:r0 + GCHUNK, GKEY + h * GDK:GKEY + (h + 1) * GDK].astype(F32)
            qe = (q * jnp.exp(b)).astype(BF16)
            ke = (k * jnp.exp(-b)).astype(BF16)
            kd = (k * jnp.exp(bl - b)).astype(BF16)
            a = jnp.where(mask, _dot_nt(qe, ke), 0.0).astype(BF16)
            items.append(dict(r0=r0, h=h, qe=qe, kd=kd, a=a, dec=jnp.exp(bl)))
    for it in items:
        v = v_ref[it["r0"]:it["r0"] + GCHUNK, it["h"] * GDV:(it["h"] + 1) * GDV]
        it["oi"] = _dot(it.pop("a"), v)
        it["u"] = _dot_tn(v, it.pop("kd"))
    return items


def _gla_kernel(qkf_ref, vf_ref, gf_ref, qkb_ref, vb_ref, gb_ref, of_ref, ob_ref, s_ref):
    @pl.when(pl.program_id(1) == 0)
    def _():
        s_ref[...] = jnp.zeros_like(s_ref)

    scans = [(_gla_local(qkf_ref, vf_ref, gf_ref, False), of_ref, 0),
             (_gla_local(qkb_ref, vb_ref, gb_ref, True), ob_ref, 1)]
    state = [[s_ref[d, h] for h in range(GH)] for d in range(2)]
    for n in range(GLA_NCH * GH):
        for items, o_ref, d in scans:
            it = items[n]
            r0, h = it["r0"], it["h"]
            s = state[d][h]
            o_ref[r0:r0 + GCHUNK, h * GDV:(h + 1) * GDV] = it["oi"] + _dot_nt(it["qe"], s.astype(BF16))
            state[d][h] = s * it["dec"] + it["u"]
    for d in range(2):
        for h in range(GH):
            s_ref[d, h] = state[d][h]


def _gla_call(qk, v, lfb):
    nxt = L // GLA_TILE

    def fwd(b, t):
        return jnp.where(t == 0, NX // GLA_TILE + b, b * nxt + t - 1)

    def bwd(b, t):
        return jnp.where(t == 0, NX // GLA_TILE + b, b * nxt + nxt - t)

    def specs(blk, d):
        return [pl.BlockSpec((GLA_TILE, 2 * GKEY), lambda b, t: (blk(b, t), 0)),
                pl.BlockSpec((GLA_TILE, GVAL), lambda b, t: (blk(b, t), 0)),
                pl.BlockSpec((GLA_TILE, GKEY), lambda b, t: (blk(b, t), d))]

    return pl.pallas_call(
        _gla_kernel,
        grid=(B, 1 + nxt),
        in_specs=specs(fwd, 0) + specs(bwd, 1),
        out_specs=[pl.BlockSpec((GLA_TILE, GVAL), lambda b, t: (fwd(b, t), 0)),
                   pl.BlockSpec((GLA_TILE, GVAL), lambda b, t: (bwd(b, t), 0))],
        out_shape=[jax.ShapeDtypeStruct((NT, GVAL), F32), jax.ShapeDtypeStruct((NT, GVAL), F32)],
        scratch_shapes=[pltpu.VMEM((2, GH, GDV, GDK), F32)],
        compiler_params=_cparams(("arbitrary", "arbitrary")),
        name="gla",
    )(qk, v, lfb, qk, v, lfb)


ATT_TQ = 1024
ATT_KC = 512
ATT_AHEAD = 2


def _flash_t(q, chunks):
    m = acc = None
    ahead = [_dot_nt(chunks[j][0], q) for j in range(min(ATT_AHEAD, len(chunks)))]
    for j, (_, vt) in enumerate(chunks):
        s = ahead.pop(0).astype(BF16)
        if j + ATT_AHEAD < len(chunks):
            ahead.append(_dot_nt(chunks[j + ATT_AHEAD][0], q))
        cm = jnp.max(s, axis=0, keepdims=True).astype(F32)
        if m is None:
            m = cm
            acc = _dot(vt, jnp.exp2(s - m.astype(BF16)))
        else:
            m_new = jnp.maximum(m, cm)
            acc = jnp.exp2(m - m_new) * acc + _dot(vt, jnp.exp2(s - m_new.astype(BF16)))
            m = m_new
    return acc[:MV, :] / acc[MV:MV + 1, :]


def _attn_kernel(q_ref, kx_ref, kc_ref, vtx_ref, vtc_ref, o_ref):
    chunks = [(kc_ref[...], vtc_ref[...])]
    for j in range(L // ATT_KC):
        chunks.append((kx_ref[j * ATT_KC:(j + 1) * ATT_KC, :], vtx_ref[:, j * ATT_KC:(j + 1) * ATT_KC]))
    o_ref[...] = _flash_t(q_ref[...], chunks).T.astype(BF16)


def _attn_ctx_kernel(q_ref, kc_ref, vtc_ref, prev_ref, o_ref):
    del prev_ref
    o_ref[...] = _flash_t(q_ref[...], [(kc_ref[...], vtc_ref[...])]).T.astype(BF16)


def _attn_call(qa, ka, vta, with_ctx):
    nq = L // ATT_TQ
    out = pl.pallas_call(
        _attn_kernel,
        grid=(B, MH, nq),
        in_specs=[pl.BlockSpec((None, ATT_TQ, MHD), lambda b, h, i: (h, b * nq + i, 0)),
                  pl.BlockSpec((None, L, MHD), lambda b, h, i: (h, b, 0)),
                  pl.BlockSpec((None, LC, MHD), lambda b, h, i: (h, NX // LC + b, 0)),
                  pl.BlockSpec((None, MVA, L), lambda b, h, i: (h, 0, b)),
                  pl.BlockSpec((None, MVA, LC), lambda b, h, i: (h, 0, NX // LC + b))],
        out_specs=pl.BlockSpec((ATT_TQ, MV), lambda b, h, i: (b * nq + i, h)),
        out_shape=jax.ShapeDtypeStruct((NT if with_ctx else NX, MH * MV), BF16),
        compiler_params=_cparams(("arbitrary", "arbitrary", "arbitrary")),
        name="attn",
    )(qa, ka, ka, vta, vta)
    if not with_ctx:
        return out
    return pl.pallas_call(
        _attn_ctx_kernel,
        grid=(B, MH),
        in_specs=[pl.BlockSpec((None, LC, MHD), lambda b, h: (h, NX // LC + b, 0)),
                  pl.BlockSpec((None, LC, MHD), lambda b, h: (h, NX // LC + b, 0)),
                  pl.BlockSpec((None, MVA, LC), lambda b, h: (h, 0, NX // LC + b)),
                  pl.BlockSpec(memory_space=pl.ANY)],
        out_specs=pl.BlockSpec((LC, MV), lambda b, h: (NX // LC + b, h)),
        out_shape=jax.ShapeDtypeStruct((NT, MH * MV), BF16),
        input_output_aliases={3: 0},
        compiler_params=_cparams(("arbitrary", "arbitrary")),
        name="attn_ctx",
    )(qa, ka, vta, out)


MERGE_TM = 512


def _merge_kernel(x_ref, of_ref, ob_ref, r_ref, gn_ref, zb_ref, u_ref, up_ref, un_ref, cw_ref,
                  c_ref, gt_ref, wa_ref, wb_ref, wc_ref, wo_ref, g5_ref, lw_ref, lb_ref, o_ref):
    tm = MERGE_TM
    c = _dot(c_ref[...], wc_ref[...])
    o = of_ref[...] + ob_ref[...]
    parts = []
    for h in range(GH):
        oh = o[:, h * GDV:(h + 1) * GDV]
        parts.append(oh * lax.rsqrt(jnp.mean(oh * oh, axis=-1, keepdims=True) + EPS))
    a_in = (jnp.concatenate(parts, axis=1) * gn_ref[...] * r_ref[...].astype(F32)).astype(BF16)
    u = u_ref[...].astype(F32)
    prev = up_ref[7:8, :].astype(F32)
    nxt = un_ref[0:1, :].astype(F32)
    loc = lax.broadcasted_iota(jnp.int32, (tm, 1), 0)
    row = loc + pl.program_id(0) * tm
    seq = jnp.where(row >= NX, LC, L)
    pos = row & (seq - 1)
    ul = jnp.where(loc == 0, prev, pltpu.roll(u, 1, 0))
    ul = jnp.where(pos == 0, 0.0, ul)
    ur = jnp.where(loc == tm - 1, nxt, pltpu.roll(u, tm - 1, 0))
    ur = jnp.where(pos == seq - 1, 0.0, ur)
    cv = ul * cw_ref[0:1, :] + u * cw_ref[1:2, :] + ur * cw_ref[2:3, :]
    b_in = (zb_ref[...].astype(F32) * cv).astype(BF16)
    for rs in _subtiles(x_ref):
        a = _dot(a_in[rs, :], wa_ref[...])
        bb = _dot(b_in[rs, :], wb_ref[...])
        mix = (gt_ref[rs, :D].astype(F32) * a + gt_ref[rs, D:2 * D].astype(F32) * bb
               + gt_ref[rs, 2 * D:].astype(F32) * c[rs, :])
        mx = _dot(mix.astype(BF16), wo_ref[...])
        z = ALPHA * x_ref[rs, :] + g5_ref[...] * mx
        o_ref[rs, :] = _layer_norm(z, lw_ref[...], lb_ref[...])


def _merge_call(x1, o_f, o_b, r, gnorm, zb, u, conv_w, c_pre, gates, wa, wb, wc, wo, g5, lw, lb, rows):
    tm = MERGE_TM
    row = lambda w: pl.BlockSpec((tm, w), lambda i: (i, 0))
    last8 = rows // 8 - 1
    return pl.pallas_call(
        _merge_kernel,
        grid=(rows // tm,),
        in_specs=[row(D), row(GVAL), row(GVAL), row(GVAL), _const((1, GVAL)),
                  row(CW), row(CW),
                  pl.BlockSpec((8, CW), lambda i: (jnp.maximum(i * (tm // 8) - 1, 0), 0)),
                  pl.BlockSpec((8, CW), lambda i: (jnp.minimum((i + 1) * (tm // 8), last8), 0)),
                  _const((3, CW)),
                  row(MH * MV), row(3 * D),
                  _const((GVAL, D)), _const((CW, D)), _const((MH * MV, D)), _const((D, D)),
                  _mod_spec(tm), _const((1, D)), _const((1, D))],
        out_specs=row(D),
        out_shape=jax.ShapeDtypeStruct((rows, D), F32),
        compiler_params=_cparams(("arbitrary",)),
        name="merge",
    )(x1, o_f, o_b, r, gnorm, zb, u, u, u, conv_w, c_pre, gates, wa, wb, wc, wo, g5, lw, lb)


def _rope_tables():
    t = jnp.arange(L, dtype=jnp.int32)
    quarter = MROPE // 4
    inv_freq = THETA ** (-jnp.arange(quarter, dtype=F32) / quarter)

    def cs(pos):
        ang = pos.astype(F32)[:, None] * inv_freq
        return jnp.cos(ang), jnp.sin(ang)

    cr, sr = cs(t // GRID_W)
    cc, sn = cs(t % GRID_W)
    tab = jnp.concatenate([cr, cr, cc, cc, -sr, sr, -sn, sn], axis=1)
    ident = jnp.concatenate([jnp.ones((NC, MROPE), F32), jnp.zeros((NC, MROPE), F32)], axis=1)
    return jnp.concatenate([jnp.tile(tab, (B, 1)), ident], axis=0)


def _swap_rope_cols(w):
    q = MROPE // 4
    return jnp.concatenate([w[..., q:2 * q], w[..., :q], w[..., 3 * q:], w[..., 2 * q:3 * q]], axis=-1)


def _layer_weights(l, w_in, gla_decay_w, gla_decay_b, mla_w_uq, mla_w_ukv):
    wi = w_in[l]
    o = 0
    cols = {}
    for name, n in (("q", GKEY), ("k", GKEY), ("v", GVAL), ("r", GVAL), ("gf", GRANK), ("gb", GRANK),
                    ("cb", CW), ("cc", CW), ("cx", CW), ("cq", MQR), ("ckv", MKVR), ("kr", MROPE),
                    ("zg", 3 * D)):
        cols[name] = wi[:, o:o + n]
        o += n
    pad = jnp.zeros((D, 128 - 2 * GRANK), F32)
    w1 = jnp.concatenate([cols["gf"], cols["gb"], pad, cols["q"], cols["k"], cols["v"], cols["r"]], axis=1)
    w2 = jnp.concatenate([cols["cb"], cols["cc"], cols["cx"]], axis=1)
    w4 = jnp.concatenate([cols["cq"], cols["ckv"], cols["kr"], _swap_rope_cols(cols["kr"])], axis=1)
    dw = jnp.zeros((128, 2 * GKEY), F32)
    dw = dw.at[:GRANK, :GKEY].set(gla_decay_w[l, 0]).at[GRANK:2 * GRANK, GKEY:].set(gla_decay_w[l, 1])
    db = jnp.concatenate([gla_decay_b[l, 0], gla_decay_b[l, 1]])[None, :]
    wq = mla_w_uq[l].reshape(MQR, MH, MNOPE + MROPE)
    wq = jnp.concatenate([wq, _swap_rope_cols(wq[..., MNOPE:])], axis=-1).reshape(MQR, MH * MHD)
    wkv = mla_w_ukv[l].reshape(MKVR, MH, MNOPE + MV)
    wk = wkv[..., :MNOPE].reshape(MKVR, MH * MNOPE)
    wvt = wkv[..., MNOPE:].reshape(MKVR, MH * MV).T
    bf = lambda a: a.astype(BF16)
    return dict(w1=bf(w1), w2=bf(w2), w3=bf(cols["zg"]), w4=bf(w4), dw=bf(dw), db=db, wq=bf(wq),
                wk=bf(wk), wvt=bf(wvt))


def kernel(x, c, ctx, c_ctx, ada_w, ada_b, ln_w, ln_b, ffn_w13, ffn_w2, w_in, b_gate, gla_decay_w,
           gla_decay_b, gla_norm, gla_proj, conv_w, conv_proj, mla_q_norm, mla_w_uq, mla_kv_norm,
           mla_w_ukv, mla_proj, w_out):
    assert x.shape == (B, L, D) and ctx.shape == (B, LC, D)
    c_all = jnp.concatenate([c, c_ctx[None, :], jnp.zeros((MODROWS - B - 1, D), F32)], axis=0)
    mods = _ada_call(c_all, ada_w, ada_b).reshape(DEPTH, MODROWS, NMOD, 1, D)
    cs = _rope_tables()
    h = jnp.concatenate([x.reshape(NX, D), ctx.reshape(NC, D)], axis=0)
    vec = lambda a: a.reshape(1, -1)
    for l in range(DEPTH):
        last = l == DEPTH - 1
        rows = NX if last else NT
        m = [mods[l, :, k] for k in range(NMOD)]
        lw = _layer_weights(l, w_in, gla_decay_w, gla_decay_b, mla_w_uq, mla_w_ukv)
        x1, hm = _ffn_call(h, NT, m[0:3], ffn_w13[l, 0].astype(BF16), ffn_w2[l, 0].astype(BF16),
                           vec(ln_w[l, 0]), vec(ln_b[l, 0]), mod_next=(m[3], m[4]))
        qk, v, r, lfb = _p1_call(hm, lw["w1"], lw["dw"], lw["db"])
        zb, u = _p2_call(hm, lw["w2"], rows)
        gates = _p3_call(hm, lw["w3"], vec(b_gate[l]), rows)
        qa, ka, vta = _p4_call(hm, lw["w4"], cs, vec(mla_q_norm[l]), lw["wq"], vec(mla_kv_norm[l]),
                               lw["wk"], lw["wvt"])
        o_f, o_b = _gla_call(qk, v, lfb)
        c_pre = _attn_call(qa, ka, vta, with_ctx=not last)
        x2 = _merge_call(x1, o_f, o_b, r, vec(gla_norm[l]), zb, u, conv_w[l], c_pre, gates,
                         gla_proj[l].astype(BF16), conv_proj[l].astype(BF16), mla_proj[l].astype(BF16),
                         w_out[l].astype(BF16), m[5], vec(ln_w[l, 1]), vec(ln_b[l, 1]), rows)
        h = _ffn_call(x2, rows, m[6:9], ffn_w13[l, 1].astype(BF16), ffn_w2[l, 1].astype(BF16),
                      vec(ln_w[l, 2]), vec(ln_b[l, 2]))
    return h.reshape(B, L, D)
```

```python
import functools

import jax
import jax.numpy as jnp
from jax import lax
from jax.experimental import pallas as pl
from jax.experimental.pallas import tpu as pltpu

F32 = jnp.float32
BF16 = jnp.bfloat16

D = 1024
B = 8
L = 4096
DEPTH = 4
LC = 256
GRID_W = 64
DFF = 2816
GH, GDK, GDV = 4, 128, 256
GKEY, GVAL = GH * GDK, GH * GDV
GRANK = 16
GTEMP = 16.0
GCHUNK = 64
CW = 1024
MH, MQR, MKVR, MNOPE, MROPE, MV = 8, 512, 256, 128, 64, 128
THETA = 10000.0
NMOD = 9
EPS = 1e-6
ALPHA = (2.0 * DEPTH) ** 0.25

NX = B * L
NC = B * LC
NT = NX + NC
MODROWS = 16
MHD = 256
MVA = MV + 16

VMEM_LIMIT = 56 * 1024 * 1024


def _cparams(sem):
    return pltpu.CompilerParams(dimension_semantics=sem, vmem_limit_bytes=VMEM_LIMIT)


def _const(shape):
    n = len(shape)
    return pl.BlockSpec(shape, lambda *_: (0,) * n, pipeline_mode=pl.Buffered(1))


def _mod_spec(tm):
    per = L // tm
    return pl.BlockSpec((None, 1, D), lambda i: (jnp.minimum(i // per, B), 0, 0))


def _silu(x):
    return x * jax.nn.sigmoid(x)


def _layer_norm(z, w, b):
    mu = jnp.mean(z, axis=-1, keepdims=True)
    zc = z - mu
    var = jnp.mean(zc * zc, axis=-1, keepdims=True)
    return zc * lax.rsqrt(var + EPS) * w + b


def _rms(x, w):
    return x * lax.rsqrt(jnp.mean(x * x, axis=-1, keepdims=True) + EPS) * w


def _dot(a, b):
    return jnp.dot(a, b, preferred_element_type=F32)


def _dot_nt(a, b):
    return lax.dot_general(a, b, (((1,), (1,)), ((), ())), preferred_element_type=F32)


def _dot_tn(a, b):
    return lax.dot_general(a, b, (((0,), (0,)), ((), ())), preferred_element_type=F32)


def _ada_kernel(c_ref, w_ref, b_ref, o_ref):
    sc = _silu(c_ref[...]).astype(BF16)
    o_ref[...] = _dot(sc, w_ref[...].astype(BF16)) + b_ref[...]


def _ada_call(c_all, ada_w, ada_b):
    tn = 1152
    return pl.pallas_call(
        _ada_kernel,
        grid=(DEPTH, NMOD * D // tn),
        in_specs=[
            pl.BlockSpec((MODROWS, D), lambda l, j: (0, 0)),
            pl.BlockSpec((None, D, tn), lambda l, j: (l, 0, j)),
            pl.BlockSpec((None, 1, tn), lambda l, j: (l, 0, j)),
        ],
        out_specs=pl.BlockSpec((None, MODROWS, tn), lambda l, j: (l, 0, j)),
        out_shape=jax.ShapeDtypeStruct((DEPTH, MODROWS, NMOD * D), F32),
        compiler_params=_cparams(("arbitrary", "arbitrary")),
        name="ada",
    )(c_all, ada_w, ada_b.reshape(DEPTH, 1, NMOD * D))


SUB_ROWS = 256
PROJ_TM = 1024

def _ffn_kernel(x_ref, *rest, emit_mod, split_tiles):
    if split_tiles:
        xc_ref, *rest = rest
        is_latent = pl.program_id(0) < split_tiles
    sh_ref, sc_ref, g_ref, w13_ref, w2_ref, lw_ref, lb_ref, *rest = rest
    for r0 in range(0, x_ref.shape[0], SUB_ROWS):
        rs = slice(r0, r0 + SUB_ROWS)
        x = x_ref[rs, :]
        if split_tiles:
            x = jnp.where(is_latent, x, xc_ref[rs, :])
        hm = (x * (1.0 + sc_ref[...]) + sh_ref[...]).astype(BF16)
        a = _dot(hm, w13_ref[:, :DFF])
        g = _dot(hm, w13_ref[:, DFF:])
        act = (_silu(a) * g).astype(BF16)
        y = _dot(act, w2_ref[...])
        z = ALPHA * x + (0.5 * g_ref[...]) * y
        out = _layer_norm(z, lw_ref[...], lb_ref[...])
        if emit_mod:
            sh2_ref, sc2_ref, o_ref, om_ref = rest
            o_ref[rs, :] = out
            om_ref[rs, :] = (out * (1.0 + sc2_ref[...]) + sh2_ref[...]).astype(BF16)
        else:
            (o_ref,) = rest
            o_ref[rs, :] = out


def _ffn_call(h, rows, mod3, w13, w2, lw, lb, mod_next=None):
    tm = 1024
    emit = mod_next is not None
    row = pl.BlockSpec((tm, D), lambda i: (i, 0))
    if isinstance(h, tuple):
        nxt = NX // tm
        tokens = list(h)
        token_specs = [pl.BlockSpec((tm, D), lambda i: (jnp.minimum(i, nxt - 1), 0)),
                       pl.BlockSpec((tm, D), lambda i: (jnp.maximum(i - nxt, 0), 0))]
    else:
        nxt = 0
        tokens = [h]
        token_specs = [row]
    in_specs = token_specs + [_mod_spec(tm), _mod_spec(tm), _mod_spec(tm),
                              _const((D, 2 * DFF)), _const((DFF, D)), _const((1, D)), _const((1, D))]
    args = [*tokens, *mod3, w13, w2, lw, lb]
    out_shape = [jax.ShapeDtypeStruct((rows, D), F32)]
    out_specs = [row]
    if emit:
        in_specs += [_mod_spec(tm), _mod_spec(tm)]
        args += list(mod_next)
        out_shape.append(jax.ShapeDtypeStruct((rows, D), BF16))
        out_specs.append(row)
    res = pl.pallas_call(
        functools.partial(_ffn_kernel, emit_mod=emit, split_tiles=nxt),
        grid=(rows // tm,),
        in_specs=in_specs,
        out_specs=out_specs,
        out_shape=out_shape,
        compiler_params=_cparams(("arbitrary",)),
        name="ffn",
    )(*args)
    return res if emit else res[0]


def _subtiles(ref, rows=SUB_ROWS):
    return [slice(r0, r0 + rows) for r0 in range(0, ref.shape[0], rows)]


def _p1_kernel(hm_ref, w_ref, dw_ref, db_ref, qk_ref, v_ref, r_ref, lfb_ref):
    n0 = 128 + 2 * GKEY
    for rs in _subtiles(hm_ref):
        hm = hm_ref[rs, :]
        z0 = _dot(hm, w_ref[:, :n0])
        xg = _dot(z0[:, :128].astype(BF16), dw_ref[...]) + db_ref[...]
        lfb_ref[rs, :] = (jnp.minimum(xg, 0.0) - jnp.log(1.0 + jnp.exp(-jnp.abs(xg)))) * (1.0 / GTEMP)
        qk_ref[rs, :GKEY] = (z0[:, 128:128 + GKEY] * (GDK ** -0.5)).astype(BF16)
        qk_ref[rs, GKEY:] = z0[:, 128 + GKEY:].astype(BF16)
        z1 = _dot(hm, w_ref[:, n0:])
        v_ref[rs, :] = z1[:, :GVAL].astype(BF16)
        r_ref[rs, :] = _silu(z1[:, GVAL:]).astype(BF16)


def _p1_call(hm, w, dw, db):
    tm = PROJ_TM
    return pl.pallas_call(
        _p1_kernel,
        grid=(NT // tm,),
        in_specs=[pl.BlockSpec((tm, D), lambda i: (i, 0)),
                  _const((D, 128 + 2 * GKEY + 2 * GVAL)),
                  _const((128, 2 * GKEY)), _const((1, 2 * GKEY))],
        out_specs=[pl.BlockSpec((tm, 2 * GKEY), lambda i: (i, 0)),
                   pl.BlockSpec((tm, GVAL), lambda i: (i, 0)),
                   pl.BlockSpec((tm, GVAL), lambda i: (i, 0)),
                   pl.BlockSpec((tm, 2 * GKEY), lambda i: (i, 0))],
        out_shape=[jax.ShapeDtypeStruct((NT, 2 * GKEY), BF16),
                   jax.ShapeDtypeStruct((NT, GVAL), BF16),
                   jax.ShapeDtypeStruct((NT, GVAL), BF16),
                   jax.ShapeDtypeStruct((NT, 2 * GKEY), F32)],
        compiler_params=_cparams(("arbitrary",)),
        name="proj_gla",
    )(hm, w, dw, db)


def _p2_kernel(hm_ref, w_ref, zb_ref, u_ref):
    for rs in _subtiles(hm_ref):
        z = _dot(hm_ref[rs, :], w_ref[...])
        zb_ref[rs, :] = z[:, :CW].astype(BF16)
        u_ref[rs, :] = (z[:, CW:2 * CW] * z[:, 2 * CW:]).astype(BF16)


def _p2_call(hm, w, rows):
    tm = PROJ_TM
    return pl.pallas_call(
        _p2_kernel,
        grid=(rows // tm,),
        in_specs=[pl.BlockSpec((tm, D), lambda i: (i, 0)), _const((D, 3 * CW))],
        out_specs=[pl.BlockSpec((tm, CW), lambda i: (i, 0)),
                   pl.BlockSpec((tm, CW), lambda i: (i, 0))],
        out_shape=[jax.ShapeDtypeStruct((rows, CW), BF16),
                   jax.ShapeDtypeStruct((rows, CW), BF16)],
        compiler_params=_cparams(("arbitrary",)),
        name="proj_conv",
    )(hm, w)


def _p3_kernel(hm_ref, w_ref, b_ref, g_ref):
    for rs in _subtiles(hm_ref):
        z = _dot(hm_ref[rs, :], w_ref[...]) + b_ref[...]
        g_ref[rs, :] = jax.nn.sigmoid(z).astype(BF16)


def _p3_call(hm, w, b_gate, rows):
    tm = PROJ_TM
    return pl.pallas_call(
        _p3_kernel,
        grid=(rows // tm,),
        in_specs=[pl.BlockSpec((tm, D), lambda i: (i, 0)), _const((D, 3 * D)), _const((1, 3 * D))],
        out_specs=pl.BlockSpec((tm, 3 * D), lambda i: (i, 0)),
        out_shape=jax.ShapeDtypeStruct((rows, 3 * D), BF16),
        compiler_params=_cparams(("arbitrary",)),
        name="proj_gate",
    )(hm, w, b_gate)


def _rope_fold(t, cs):
    rr = t * cs
    return rr + pltpu.roll(rr, MROPE, 1)


Q_SCALE = (MNOPE + MROPE) ** -0.5 * 1.4426950408889634
MLA_SUB = 512


def _p4_kernel(hm_ref, w_ref, cs_ref, qn_ref, wuq_ref, kvn_ref, wuk_ref, wuvt_ref, q_ref, k_ref, vt_ref):
    for rs in _subtiles(hm_ref, MLA_SUB):
        z = _dot(hm_ref[rs, :], w_ref[...])
        cs = cs_ref[rs, :]
        cqn = _rms(z[:, :MQR], qn_ref[...]).astype(BF16)
        kvn = _rms(z[:, MQR:MQR + MKVR], kvn_ref[...]).astype(BF16)
        qz = _dot(cqn, wuq_ref[...]) * Q_SCALE
        kz = _dot(kvn, wuk_ref[...])
        vt = _dot_nt(wuvt_ref[...], kvn)
        for h in range(MH):
            base = h * MHD
            rot = _rope_fold(qz[:, base + MNOPE:base + MHD], cs)
            q_ref[h, rs, :] = jnp.concatenate([qz[:, base:base + MNOPE], rot], axis=1).astype(BF16)
        krot = _rope_fold(z[:, MQR + MKVR:], cs)
        lane = lax.broadcasted_iota(jnp.int32, krot.shape, 1)
        krot = jnp.where(lane < MROPE, krot, 0.0)
        for h in range(MH):
            k_ref[h, rs, :] = jnp.concatenate([kz[:, h * MNOPE:(h + 1) * MNOPE], krot], axis=1).astype(BF16)
            vt_ref[h, :MV, rs] = vt[h * MV:(h + 1) * MV, :].astype(BF16)
            vt_ref[h, MV:, rs] = jnp.ones((MVA - MV, MLA_SUB), BF16)


def _p4_call(hm, w, cs, qn, wuq, kvn, wuk, wuvt):
    tm = PROJ_TM
    return pl.pallas_call(
        _p4_kernel,
        grid=(NT // tm,),
        in_specs=[pl.BlockSpec((tm, D), lambda i: (i, 0)), _const((D, MQR + MKVR + 2 * MROPE)),
                  pl.BlockSpec((tm, 2 * MROPE), lambda i: (i, 0)),
                  _const((1, MQR)), _const((MQR, MH * MHD)),
                  _const((1, MKVR)), _const((MKVR, MH * MNOPE)), _const((MH * MV, MKVR))],
        out_specs=[pl.BlockSpec((MH, tm, MHD), lambda i: (0, i, 0)),
                   pl.BlockSpec((MH, tm, MHD), lambda i: (0, i, 0)),
                   pl.BlockSpec((MH, MVA, tm), lambda i: (0, 0, i))],
        out_shape=[jax.ShapeDtypeStruct((MH, NT, MHD), BF16),
                   jax.ShapeDtypeStruct((MH, NT, MHD), BF16),
                   jax.ShapeDtypeStruct((MH, MVA, NT), BF16)],
        compiler_params=_cparams(("arbitrary",)),
        name="proj_mla",
    )(hm, w, cs, qn, wuq, kvn, wuk, wuvt)


GLA_TILE = 256
GLA_NCH = GLA_TILE // GCHUNK


def _gla_local(qk_ref, v_ref, g_ref, backward):
    g = g_ref[...]
    ri = lax.broadcasted_iota(jnp.int32, (GLA_TILE, GLA_TILE), 0)
    ci = lax.broadcasted_iota(jnp.int32, (GLA_TILE, GLA_TILE), 1)
    same = (ri // GCHUNK) == (ci // GCHUNK)
    tri = jnp.where(same & ((ci >= ri) if backward else (ci <= ri)), 1.0, 0.0).astype(BF16)
    g_hi = g.astype(BF16)
    g_lo = (g - g_hi.astype(F32)).astype(BF16)
    bc = _dot(tri, g_hi) + _dot(tri, g_lo)
    r64 = lax.broadcasted_iota(jnp.int32, (GCHUNK, GCHUNK), 0)
    c64 = lax.broadcasted_iota(jnp.int32, (GCHUNK, GCHUNK), 1)
    mask = (c64 >= r64) if backward else (c64 <= r64)
    order = range(GLA_NCH - 1, -1, -1) if backward else range(GLA_NCH)
    items = []
    for c in order:
        r0 = c * GCHUNK
        for h in range(GH):
            b = bc[r0:r0 + GCHUNK, h * GDK:(h + 1) * GDK]
            bl = b[0:1] if backward else b[GCHUNK - 1:GCHUNK]
            q = qk_ref[r0:r0 + GCHUNK, h * GDK:(h + 1) * GDK].astype(F32)
            k = qk_ref[r0:r0 + GCHUNK, GKEY + h * GDK:GKEY + (h + 1) * GDK].astype(F32)
            qe = (q * jnp.exp(b)).astype(BF16)
            ke = (k * jnp.exp(-b)).astype(BF16)
            kd = (k * jnp.exp(bl - b)).astype(BF16)
            a = jnp.where(mask, _dot_nt(qe, ke), 0.0).astype(BF16)
            items.append(dict(r0=r0, h=h, qe=qe, kd=kd, a=a, dec=jnp.exp(bl)))
    for it in items:
        v = v_ref[it["r0"]:it["r0"] + GCHUNK, it["h"] * GDV:(it["h"] + 1) * GDV]
        it["oi"] = _dot(it.pop("a"), v)
        it["u"] = _dot_tn(v, it.pop("kd"))
    return items


def _gla_kernel(qkf_ref, vf_ref, gf_ref, qkb_ref, vb_ref, gb_ref, of_ref, ob_ref, s_ref):
    @pl.when(pl.program_id(1) == 0)
    def _():
        s_ref[...] = jnp.zeros_like(s_ref)

    scans = [(_gla_local(qkf_ref, vf_ref, gf_ref, False), of_ref, 0),
             (_gla_local(qkb_ref, vb_ref, gb_ref, True), ob_ref, 1)]
    state = [[s_ref[d, h] for h in range(GH)] for d in range(2)]
    for n in range(GLA_NCH * GH):
        for items, o_ref, d in scans:
            it = items[n]
            r0, h = it["r0"], it["h"]
            s = state[d][h]
            o_ref[r0:r0 + GCHUNK, h * GDV:(h + 1) * GDV] = it["oi"] + _dot_nt(it["qe"], s.astype(BF16))
            state[d][h] = s * it["dec"] + it["u"]
    for d in range(2):
        for h in range(GH):
            s_ref[d, h] = state[d][h]


def _gla_call(qk, v, lfb):
    nxt = L // GLA_TILE

    def fwd(b, t):
        return jnp.where(t == 0, NX // GLA_TILE + b, b * nxt + t - 1)

    def bwd(b, t):
        return jnp.where(t == 0, NX // GLA_TILE + b, b * nxt + nxt - t)

    def specs(blk, d):
        return [pl.BlockSpec((GLA_TILE, 2 * GKEY), lambda b, t: (blk(b, t), 0)),
                pl.BlockSpec((GLA_TILE, GVAL), lambda b, t: (blk(b, t), 0)),
                pl.BlockSpec((GLA_TILE, GKEY), lambda b, t: (blk(b, t), d))]

    return pl.pallas_call(
        _gla_kernel,
        grid=(B, 1 + nxt),
        in_specs=specs(fwd, 0) + specs(bwd, 1),
        out_specs=[pl.BlockSpec((GLA_TILE, GVAL), lambda b, t: (fwd(b, t), 0)),
                   pl.BlockSpec((GLA_TILE, GVAL), lambda b, t: (bwd(b, t), 0))],
        out_shape=[jax.ShapeDtypeStruct((NT, GVAL), F32), jax.ShapeDtypeStruct((NT, GVAL), F32)],
        scratch_shapes=[pltpu.VMEM((2, GH, GDV, GDK), F32)],
        compiler_params=_cparams(("arbitrary", "arbitrary")),
        name="gla",
    )(qk, v, lfb, qk, v, lfb)


ATT_TQ = 1024
ATT_KC = 512
ATT_AHEAD = 2
ATT_HP = 2


def _flash_t(streams):
    n = len(streams[0][1])
    m = [None] * len(streams)
    acc = [None] * len(streams)
    ahead = [[_dot_nt(ch[j][0], q) for j in range(min(ATT_AHEAD, n))] for q, ch in streams]
    for j in range(n):
        for i, (q, ch) in enumerate(streams):
            s = ahead[i].pop(0)
            if j + ATT_AHEAD < n:
                ahead[i].append(_dot_nt(ch[j + ATT_AHEAD][0], q))
            vt = ch[j][1]
            cm = jnp.max(s, axis=0, keepdims=True)
            if j == 0:
                m[i] = cm
                acc[i] = _dot(vt, jnp.exp2(s - cm).astype(BF16))
            else:
                m_new = jnp.maximum(m[i], cm)
                acc[i] = jnp.exp2(m[i] - m_new) * acc[i] + _dot(vt, jnp.exp2(s - m_new).astype(BF16))
                m[i] = m_new
    return [a[:MV, :] / a[MV:MV + 1, :] for a in acc]


def _attn_kernel(q_ref, kx_ref, kc_ref, vtx_ref, vtc_ref, o_ref):
    streams = []
    for h in range(ATT_HP):
        chunks = [(kc_ref[h], vtc_ref[h])]
        for j in range(L // ATT_KC):
            ks = slice(j * ATT_KC, (j + 1) * ATT_KC)
            chunks.append((kx_ref[h, ks, :], vtx_ref[h, :, ks]))
        streams.append((q_ref[h], chunks))
    for h, o in enumerate(_flash_t(streams)):
        o_ref[:, h * MV:(h + 1) * MV] = o.T.astype(BF16)


def _attn_ctx_kernel(q_ref, kc_ref, vtc_ref, prev_ref, o_ref):
    del prev_ref
    outs = _flash_t([(q_ref[h], [(kc_ref[h], vtc_ref[h])]) for h in range(MH)])
    for h, o in enumerate(outs):
        o_ref[:, h * MV:(h + 1) * MV] = o.T.astype(BF16)


def _attn_call(qa, ka, vta, with_ctx):
    nq = L // ATT_TQ
    hp = ATT_HP
    out = pl.pallas_call(
        _attn_kernel,
        grid=(B, MH // hp, nq),
        in_specs=[pl.BlockSpec((hp, ATT_TQ, MHD), lambda b, h, i: (h, b * nq + i, 0)),
                  pl.BlockSpec((hp, L, MHD), lambda b, h, i: (h, b, 0)),
                  pl.BlockSpec((hp, LC, MHD), lambda b, h, i: (h, NX // LC + b, 0)),
                  pl.BlockSpec((hp, MVA, L), lambda b, h, i: (h, 0, b)),
                  pl.BlockSpec((hp, MVA, LC), lambda b, h, i: (h, 0, NX // LC + b))],
        out_specs=pl.BlockSpec((ATT_TQ, hp * MV), lambda b, h, i: (b * nq + i, h)),
        out_shape=jax.ShapeDtypeStruct((NT if with_ctx else NX, MH * MV), BF16),
        compiler_params=_cparams(("arbitrary", "arbitrary", "arbitrary")),
        name="attn",
    )(qa, ka, ka, vta, vta)
    if not with_ctx:
        return out
    return pl.pallas_call(
        _attn_ctx_kernel,
        grid=(B,),
        in_specs=[pl.BlockSpec((MH, LC, MHD), lambda b: (0, NX // LC + b, 0)),
                  pl.BlockSpec((MH, LC, MHD), lambda b: (0, NX // LC + b, 0)),
                  pl.BlockSpec((MH, MVA, LC), lambda b: (0, 0, NX // LC + b)),
                  pl.BlockSpec(memory_space=pl.ANY)],
        out_specs=pl.BlockSpec((LC, MH * MV), lambda b: (NX // LC + b, 0)),
        out_shape=jax.ShapeDtypeStruct((NT, MH * MV), BF16),
        input_output_aliases={3: 0},
        compiler_params=_cparams(("arbitrary",)),
        name="attn_ctx",
    )(qa, ka, vta, out)


MERGE_TM = 512


def _merge_kernel(x_ref, of_ref, ob_ref, r_ref, gn_ref, zb_ref, u_ref, up_ref, un_ref, cw_ref,
                  c_ref, gt_ref, wa_ref, wb_ref, wc_ref, wo_ref, g5_ref, lw_ref, lb_ref, o_ref):
    tm = MERGE_TM
    c = _dot(c_ref[...], wc_ref[...])
    o = of_ref[...] + ob_ref[...]
    parts = []
    for h in range(GH):
        oh = o[:, h * GDV:(h + 1) * GDV]
        parts.append(oh * lax.rsqrt(jnp.mean(oh * oh, axis=-1, keepdims=True) + EPS))
    a_in = (jnp.concatenate(parts, axis=1) * gn_ref[...] * r_ref[...].astype(F32)).astype(BF16)
    u = u_ref[...].astype(F32)
    prev = up_ref[7:8, :].astype(F32)
    nxt = un_ref[0:1, :].astype(F32)
    loc = lax.broadcasted_iota(jnp.int32, (tm, 1), 0)
    row = loc + pl.program_id(0) * tm
    seq = jnp.where(row >= NX, LC, L)
    pos = row & (seq - 1)
    ul = jnp.where(loc == 0, prev, pltpu.roll(u, 1, 0))
    ul = jnp.where(pos == 0, 0.0, ul)
    ur = jnp.where(loc == tm - 1, nxt, pltpu.roll(u, tm - 1, 0))
    ur = jnp.where(pos == seq - 1, 0.0, ur)
    cv = ul * cw_ref[0:1, :] + u * cw_ref[1:2, :] + ur * cw_ref[2:3, :]
    b_in = (zb_ref[...].astype(F32) * cv).astype(BF16)
    subs = _subtiles(x_ref)
    ab = [(_dot(a_in[rs, :], wa_ref[...]), _dot(b_in[rs, :], wb_ref[...])) for rs in subs]
    for rs, (a, bb) in zip(subs, ab):
        mix = (gt_ref[rs, :D].astype(F32) * a + gt_ref[rs, D:2 * D].astype(F32) * bb
               + gt_ref[rs, 2 * D:].astype(F32) * c[rs, :])
        mx = _dot(mix.astype(BF16), wo_ref[...])
        z = ALPHA * x_ref[rs, :] + g5_ref[...] * mx
        o_ref[rs, :] = _layer_norm(z, lw_ref[...], lb_ref[...])


def _merge_call(x1, o_f, o_b, r, gnorm, zb, u, conv_w, c_pre, gates, wa, wb, wc, wo, g5, lw, lb, rows):
    tm = MERGE_TM
    row = lambda w: pl.BlockSpec((tm, w), lambda i: (i, 0))
    last8 = rows // 8 - 1
    return pl.pallas_call(
        _merge_kernel,
        grid=(rows // tm,),
        in_specs=[row(D), row(GVAL), row(GVAL), row(GVAL), _const((1, GVAL)),
                  row(CW), row(CW),
                  pl.BlockSpec((8, CW), lambda i: (jnp.maximum(i * (tm // 8) - 1, 0), 0)),
                  pl.BlockSpec((8, CW), lambda i: (jnp.minimum((i + 1) * (tm // 8), last8), 0)),
                  _const((3, CW)),
                  row(MH * MV), row(3 * D),
                  _const((GVAL, D)), _const((CW, D)), _const((MH * MV, D)), _const((D, D)),
                  _mod_spec(tm), _const((1, D)), _const((1, D))],
        out_specs=row(D),
        out_shape=jax.ShapeDtypeStruct((rows, D), F32),
        compiler_params=_cparams(("arbitrary",)),
        name="merge",
    )(x1, o_f, o_b, r, gnorm, zb, u, u, u, conv_w, c_pre, gates, wa, wb, wc, wo, g5, lw, lb)


def _rope_tables():
    t = jnp.arange(L, dtype=jnp.int32)
    quarter = MROPE // 4
    inv_freq = THETA ** (-jnp.arange(quarter, dtype=F32) / quarter)

    def cs(pos):
        ang = pos.astype(F32)[:, None] * inv_freq
        return jnp.cos(ang), jnp.sin(ang)

    cr, sr = cs(t // GRID_W)
    cc, sn = cs(t % GRID_W)
    tab = jnp.concatenate([cr, cr, cc, cc, -sr, sr, -sn, sn], axis=1)
    ident = jnp.concatenate([jnp.ones((NC, MROPE), F32), jnp.zeros((NC, MROPE), F32)], axis=1)
    return jnp.concatenate([jnp.tile(tab, (B, 1)), ident], axis=0)


def _swap_rope_cols(w):
    q = MROPE // 4
    return jnp.concatenate([w[..., q:2 * q], w[..., :q], w[..., 3 * q:], w[..., 2 * q:3 * q]], axis=-1)


def _layer_weights(l, w_in, gla_decay_w, gla_decay_b, mla_w_uq, mla_w_ukv):
    wi = w_in[l]
    o = 0
    cols = {}
    for name, n in (("q", GKEY), ("k", GKEY), ("v", GVAL), ("r", GVAL), ("gf", GRANK), ("gb", GRANK),
                    ("cb", CW), ("cc", CW), ("cx", CW), ("cq", MQR), ("ckv", MKVR), ("kr", MROPE),
                    ("zg", 3 * D)):
        cols[name] = wi[:, o:o + n]
        o += n
    pad = jnp.zeros((D, 128 - 2 * GRANK), F32)
    w1 = jnp.concatenate([cols["gf"], cols["gb"], pad, cols["q"], cols["k"], cols["v"], cols["r"]], axis=1)
    w2 = jnp.concatenate([cols["cb"], cols["cc"], cols["cx"]], axis=1)
    w4 = jnp.concatenate([cols["cq"], cols["ckv"], cols["kr"], _swap_rope_cols(cols["kr"])], axis=1)
    dw = jnp.zeros((128, 2 * GKEY), F32)
    dw = dw.at[:GRANK, :GKEY].set(gla_decay_w[l, 0]).at[GRANK:2 * GRANK, GKEY:].set(gla_decay_w[l, 1])
    db = jnp.concatenate([gla_decay_b[l, 0], gla_decay_b[l, 1]])[None, :]
    wq = mla_w_uq[l].reshape(MQR, MH, MNOPE + MROPE)
    wq = jnp.concatenate([wq, _swap_rope_cols(wq[..., MNOPE:])], axis=-1).reshape(MQR, MH * MHD)
    wkv = mla_w_ukv[l].reshape(MKVR, MH, MNOPE + MV)
    wk = wkv[..., :MNOPE].reshape(MKVR, MH * MNOPE)
    wvt = wkv[..., MNOPE:].reshape(MKVR, MH * MV).T
    bf = lambda a: a.astype(BF16)
    return dict(w1=bf(w1), w2=bf(w2), w3=bf(cols["zg"]), w4=bf(w4), dw=bf(dw), db=db, wq=bf(wq),
                wk=bf(wk), wvt=bf(wvt))


def kernel(x, c, ctx, c_ctx, ada_w, ada_b, ln_w, ln_b, ffn_w13, ffn_w2, w_in, b_gate, gla_decay_w,
           gla_decay_b, gla_norm, gla_proj, conv_w, conv_proj, mla_q_norm, mla_w_uq, mla_kv_norm,
           mla_w_ukv, mla_proj, w_out):
    assert x.shape == (B, L, D) and ctx.shape == (B, LC, D)
    c_all = jnp.concatenate([c, c_ctx[None, :], jnp.zeros((MODROWS - B - 1, D), F32)], axis=0)
    mods = _ada_call(c_all, ada_w, ada_b).reshape(DEPTH, MODROWS, NMOD, 1, D)
    cs = _rope_tables()
    h = (x.reshape(NX, D), ctx.reshape(NC, D))
    vec = lambda a: a.reshape(1, -1)
    for l in range(DEPTH):
        last = l == DEPTH - 1
        rows = NX if last else NT
        m = [mods[l, :, k] for k in range(NMOD)]
        lw = _layer_weights(l, w_in, gla_decay_w, gla_decay_b, mla_w_uq, mla_w_ukv)
        x1, hm = _ffn_call(h, NT, m[0:3], ffn_w13[l, 0].astype(BF16), ffn_w2[l, 0].astype(BF16),
                           vec(ln_w[l, 0]), vec(ln_b[l, 0]), mod_next=(m[3], m[4]))
        qk, v, r, lfb = _p1_call(hm, lw["w1"], lw["dw"], lw["db"])
        zb, u = _p2_call(hm, lw["w2"], rows)
        gates = _p3_call(hm, lw["w3"], vec(b_gate[l]), rows)
        qa, ka, vta = _p4_call(hm, lw["w4"], cs, vec(mla_q_norm[l]), lw["wq"], vec(mla_kv_norm[l]),
                               lw["wk"], lw["wvt"])
        o_f, o_b = _gla_call(qk, v, lfb)
        c_pre = _attn_call(qa, ka, vta, with_ctx=not last)
        x2 = _merge_call(x1, o_f, o_b, r, vec(gla_norm[l]), zb, u, conv_w[l], c_pre, gates,
                         gla_proj[l].astype(BF16), conv_proj[l].astype(BF16), mla_proj[l].astype(BF16),
                         w_out[l].astype(BF16), m[5], vec(ln_w[l, 1]), vec(ln_b[l, 1]), rows)
        h = _ffn_call(x2, rows, m[6:9], ffn_w13[l, 1].astype(BF16), ffn_w2[l, 1].astype(BF16),
                      vec(ln_w[l, 2]), vec(ln_b[l, 2]))
    return h.reshape(B, L, D)
```

```python
import functools

import jax
import jax.numpy as jnp
from jax import lax
from jax.experimental import pallas as pl
from jax.experimental.pallas import tpu as pltpu

F32 = jnp.float32
BF16 = jnp.bfloat16

D = 1024
B = 8
L = 4096
DEPTH = 4
LC = 256
GRID_W = 64
DFF = 2816
GH, GDK, GDV = 4, 128, 256
GKEY, GVAL = GH * GDK, GH * GDV
GRANK = 16
GTEMP = 16.0
GCHUNK = 64
CW = 1024
MH, MQR, MKVR, MNOPE, MROPE, MV = 8, 512, 256, 128, 64, 128
THETA = 10000.0
NMOD = 9
EPS = 1e-6
ALPHA = (2.0 * DEPTH) ** 0.25

NX = B * L
NC = B * LC
NT = NX + NC
MODROWS = 16
MHD = 256
MVA = MV + 16

VMEM_LIMIT = 56 * 1024 * 1024


def _cparams(sem):
    return pltpu.CompilerParams(dimension_semantics=sem, vmem_limit_bytes=VMEM_LIMIT)


def _const(shape):
    n = len(shape)
    return pl.BlockSpec(shape, lambda *_: (0,) * n, pipeline_mode=pl.Buffered(1))


def _mod_spec(tm):
    per = L // tm
    return pl.BlockSpec((None, 1, D), lambda i: (jnp.minimum(i // per, B), 0, 0))


def _silu(x):
    return x * jax.nn.sigmoid(x)


def _layer_norm(z, w, b):
    mu = jnp.mean(z, axis=-1, keepdims=True)
    zc = z - mu
    var = jnp.mean(zc * zc, axis=-1, keepdims=True)
    return zc * lax.rsqrt(var + EPS) * w + b


def _rms(x, w):
    return x * lax.rsqrt(jnp.mean(x * x, axis=-1, keepdims=True) + EPS) * w


def _dot(a, b):
    return jnp.dot(a, b, preferred_element_type=F32)


def _dot_nt(a, b):
    return lax.dot_general(a, b, (((1,), (1,)), ((), ())), preferred_element_type=F32)


def _dot_tn(a, b):
    return lax.dot_general(a, b, (((0,), (0,)), ((), ())), preferred_element_type=F32)


def _ada_kernel(c_ref, w_ref, b_ref, o_ref):
    sc = _silu(c_ref[...]).astype(BF16)
    o_ref[...] = _dot(sc, w_ref[...].astype(BF16)) + b_ref[...]


def _ada_call(c_all, ada_w, ada_b):
    tn = 1152
    return pl.pallas_call(
        _ada_kernel,
        grid=(DEPTH, NMOD * D // tn),
        in_specs=[
            pl.BlockSpec((MODROWS, D), lambda l, j: (0, 0)),
            pl.BlockSpec((None, D, tn), lambda l, j: (l, 0, j)),
            pl.BlockSpec((None, 1, tn), lambda l, j: (l, 0, j)),
        ],
        out_specs=pl.BlockSpec((None, MODROWS, tn), lambda l, j: (l, 0, j)),
        out_shape=jax.ShapeDtypeStruct((DEPTH, MODROWS, NMOD * D), F32),
        compiler_params=_cparams(("arbitrary", "arbitrary")),
        name="ada",
    )(c_all, ada_w, ada_b.reshape(DEPTH, 1, NMOD * D))


SUB_ROWS = 256
PROJ_TM = 1024

def _ffn_kernel(x_ref, *rest, emit_mod, split_tiles):
    if split_tiles:
        xc_ref, *rest = rest
        is_latent = pl.program_id(0) < split_tiles
    sh_ref, sc_ref, g_ref, w13_ref, w2_ref, lw_ref, lb_ref, *rest = rest
    for r0 in range(0, x_ref.shape[0], SUB_ROWS):
        rs = slice(r0, r0 + SUB_ROWS)
        x = x_ref[rs, :]
        if split_tiles:
            x = jnp.where(is_latent, x, xc_ref[rs, :])
        hm = (x * (1.0 + sc_ref[...]) + sh_ref[...]).astype(BF16)
        a = _dot(hm, w13_ref[:, :DFF])
        g = _dot(hm, w13_ref[:, DFF:])
        act = (_silu(a) * g).astype(BF16)
        y = _dot(act, w2_ref[...])
        z = ALPHA * x + (0.5 * g_ref[...]) * y
        out = _layer_norm(z, lw_ref[...], lb_ref[...])
        if emit_mod:
            sh2_ref, sc2_ref, o_ref, om_ref = rest
            o_ref[rs, :] = out
            om_ref[rs, :] = (out * (1.0 + sc2_ref[...]) + sh2_ref[...]).astype(BF16)
        else:
            (o_ref,) = rest
            o_ref[rs, :] = out


def _ffn_call(h, rows, mod3, w13, w2, lw, lb, mod_next=None):
    tm = 1024
    emit = mod_next is not None
    row = pl.BlockSpec((tm, D), lambda i: (i, 0))
    if isinstance(h, tuple):
        nxt = NX // tm
        tokens = list(h)
        token_specs = [pl.BlockSpec((tm, D), lambda i: (jnp.minimum(i, nxt - 1), 0)),
                       pl.BlockSpec((tm, D), lambda i: (jnp.maximum(i - nxt, 0), 0))]
    else:
        nxt = 0
        tokens = [h]
        token_specs = [row]
    in_specs = token_specs + [_mod_spec(tm), _mod_spec(tm), _mod_spec(tm),
                              _const((D, 2 * DFF)), _const((DFF, D)), _const((1, D)), _const((1, D))]
    args = [*tokens, *mod3, w13, w2, lw, lb]
    out_shape = [jax.ShapeDtypeStruct((rows, D), F32)]
    out_specs = [row]
    if emit:
        in_specs += [_mod_spec(tm), _mod_spec(tm)]
        args += list(mod_next)
        out_shape.append(jax.ShapeDtypeStruct((rows, D), BF16))
        out_specs.append(row)
    res = pl.pallas_call(
        functools.partial(_ffn_kernel, emit_mod=emit, split_tiles=nxt),
        grid=(rows // tm,),
        in_specs=in_specs,
        out_specs=out_specs,
        out_shape=out_shape,
        compiler_params=_cparams(("arbitrary",)),
        name="ffn",
    )(*args)
    return res if emit else res[0]


def _subtiles(ref, rows=SUB_ROWS):
    return [slice(r0, r0 + rows) for r0 in range(0, ref.shape[0], rows)]


def _p1_kernel(hm_ref, w_ref, dw_ref, db_ref, qk_ref, v_ref, r_ref, lfb_ref):
    n0 = 128 + 2 * GKEY
    for rs in _subtiles(hm_ref):
        hm = hm_ref[rs, :]
        z0 = _dot(hm, w_ref[:, :n0])
        xg = _dot(z0[:, :128].astype(BF16), dw_ref[...]) + db_ref[...]
        lfb_ref[rs, :] = (jnp.minimum(xg, 0.0) - jnp.log(1.0 + jnp.exp(-jnp.abs(xg)))) * (1.0 / GTEMP)
        qk_ref[rs, :GKEY] = (z0[:, 128:128 + GKEY] * (GDK ** -0.5)).astype(BF16)
        qk_ref[rs, GKEY:] = z0[:, 128 + GKEY:].astype(BF16)
        z1 = _dot(hm, w_ref[:, n0:])
        v_ref[rs, :] = z1[:, :GVAL].astype(BF16)
        r_ref[rs, :] = _silu(z1[:, GVAL:]).astype(BF16)


def _p1_call(hm, w, dw, db):
    tm = PROJ_TM
    return pl.pallas_call(
        _p1_kernel,
        grid=(NT // tm,),
        in_specs=[pl.BlockSpec((tm, D), lambda i: (i, 0)),
                  _const((D, 128 + 2 * GKEY + 2 * GVAL)),
                  _const((128, 2 * GKEY)), _const((1, 2 * GKEY))],
        out_specs=[pl.BlockSpec((tm, 2 * GKEY), lambda i: (i, 0)),
                   pl.BlockSpec((tm, GVAL), lambda i: (i, 0)),
                   pl.BlockSpec((tm, GVAL), lambda i: (i, 0)),
                   pl.BlockSpec((tm, 2 * GKEY), lambda i: (i, 0))],
        out_shape=[jax.ShapeDtypeStruct((NT, 2 * GKEY), BF16),
                   jax.ShapeDtypeStruct((NT, GVAL), BF16),
                   jax.ShapeDtypeStruct((NT, GVAL), BF16),
                   jax.ShapeDtypeStruct((NT, 2 * GKEY), F32)],
        compiler_params=_cparams(("arbitrary",)),
        name="proj_gla",
    )(hm, w, dw, db)


def _p2_kernel(hm_ref, w_ref, zb_ref, u_ref):
    for rs in _subtiles(hm_ref):
        z = _dot(hm_ref[rs, :], w_ref[...])
        zb_ref[rs, :] = z[:, :CW].astype(BF16)
        u_ref[rs, :] = (z[:, CW:2 * CW] * z[:, 2 * CW:]).astype(BF16)


def _p2_call(hm, w, rows):
    tm = PROJ_TM
    return pl.pallas_call(
        _p2_kernel,
        grid=(rows // tm,),
        in_specs=[pl.BlockSpec((tm, D), lambda i: (i, 0)), _const((D, 3 * CW))],
        out_specs=[pl.BlockSpec((tm, CW), lambda i: (i, 0)),
                   pl.BlockSpec((tm, CW), lambda i: (i, 0))],
        out_shape=[jax.ShapeDtypeStruct((rows, CW), BF16),
                   jax.ShapeDtypeStruct((rows, CW), BF16)],
        compiler_params=_cparams(("arbitrary",)),
        name="proj_conv",
    )(hm, w)


def _p3_kernel(hm_ref, w_ref, b_ref, g_ref):
    for rs in _subtiles(hm_ref):
        z = _dot(hm_ref[rs, :], w_ref[...]) + b_ref[...]
        g_ref[rs, :] = jax.nn.sigmoid(z).astype(BF16)


def _p3_call(hm, w, b_gate, rows):
    tm = PROJ_TM
    return pl.pallas_call(
        _p3_kernel,
        grid=(rows // tm,),
        in_specs=[pl.BlockSpec((tm, D), lambda i: (i, 0)), _const((D, 3 * D)), _const((1, 3 * D))],
        out_specs=pl.BlockSpec((tm, 3 * D), lambda i: (i, 0)),
        out_shape=jax.ShapeDtypeStruct((rows, 3 * D), BF16),
        compiler_params=_cparams(("arbitrary",)),
        name="proj_gate",
    )(hm, w, b_gate)


def _rope_fold(t, cs):
    rr = t * cs
    return rr + pltpu.roll(rr, MROPE, 1)


Q_SCALE = (MNOPE + MROPE) ** -0.5 * 1.4426950408889634
MLA_SUB = 512


def _p4_kernel(hm_ref, w_ref, cs_ref, qn_ref, wuq_ref, kvn_ref, wuk_ref, wuvt_ref, q_ref, k_ref, vt_ref):
    for rs in _subtiles(hm_ref, MLA_SUB):
        z = _dot(hm_ref[rs, :], w_ref[...])
        cs = cs_ref[rs, :]
        cqn = _rms(z[:, :MQR], qn_ref[...]).astype(BF16)
        kvn = _rms(z[:, MQR:MQR + MKVR], kvn_ref[...]).astype(BF16)
        qz = _dot(cqn, wuq_ref[...]) * Q_SCALE
        kz = _dot(kvn, wuk_ref[...])
        vt = _dot_nt(wuvt_ref[...], kvn)
        for h in range(MH):
            base = h * MHD
            rot = _rope_fold(qz[:, base + MNOPE:base + MHD], cs)
            q_ref[h, rs, :] = jnp.concatenate([qz[:, base:base + MNOPE], rot], axis=1).astype(BF16)
        krot = _rope_fold(z[:, MQR + MKVR:], cs)
        lane = lax.broadcasted_iota(jnp.int32, krot.shape, 1)
        krot = jnp.where(lane < MROPE, krot, 0.0)
        for h in range(MH):
            k_ref[h, rs, :] = jnp.concatenate([kz[:, h * MNOPE:(h + 1) * MNOPE], krot], axis=1).astype(BF16)
            vt_ref[h, :MV, rs] = vt[h * MV:(h + 1) * MV, :].astype(BF16)
            vt_ref[h, MV:, rs] = jnp.ones((MVA - MV, MLA_SUB), BF16)


def _p4_call(hm, w, cs, qn, wuq, kvn, wuk, wuvt):
    tm = PROJ_TM
    return pl.pallas_call(
        _p4_kernel,
        grid=(NT // tm,),
        in_specs=[pl.BlockSpec((tm, D), lambda i: (i, 0)), _const((D, MQR + MKVR + 2 * MROPE)),
                  pl.BlockSpec((tm, 2 * MROPE), lambda i: (i, 0)),
                  _const((1, MQR)), _const((MQR, MH * MHD)),
                  _const((1, MKVR)), _const((MKVR, MH * MNOPE)), _const((MH * MV, MKVR))],
        out_specs=[pl.BlockSpec((MH, tm, MHD), lambda i: (0, i, 0)),
                   pl.BlockSpec((MH, tm, MHD), lambda i: (0, i, 0)),
                   pl.BlockSpec((MH, MVA, tm), lambda i: (0, 0, i))],
        out_shape=[jax.ShapeDtypeStruct((MH, NT, MHD), BF16),
                   jax.ShapeDtypeStruct((MH, NT, MHD), BF16),
                   jax.ShapeDtypeStruct((MH, MVA, NT), BF16)],
        compiler_params=_cparams(("arbitrary",)),
        name="proj_mla",
    )(hm, w, cs, qn, wuq, kvn, wuk, wuvt)


GLA_TILE = 256
GLA_NCH = GLA_TILE // GCHUNK


def _gla_local(qk_ref, v_ref, g_ref, backward):
    g = g_ref[...]
    ri = lax.broadcasted_iota(jnp.int32, (GLA_TILE, GLA_TILE), 0)
    ci = lax.broadcasted_iota(jnp.int32, (GLA_TILE, GLA_TILE), 1)
    same = (ri // GCHUNK) == (ci // GCHUNK)
    tri = jnp.where(same & ((ci >= ri) if backward else (ci <= ri)), 1.0, 0.0).astype(BF16)
    g_hi = g.astype(BF16)
    g_lo = (g - g_hi.astype(F32)).astype(BF16)
    bc = _dot(tri, g_hi) + _dot(tri, g_lo)
    r64 = lax.broadcasted_iota(jnp.int32, (GCHUNK, GCHUNK), 0)
    c64 = lax.broadcasted_iota(jnp.int32, (GCHUNK, GCHUNK), 1)
    mask = (c64 >= r64) if backward else (c64 <= r64)
    order = range(GLA_NCH - 1, -1, -1) if backward else range(GLA_NCH)
    items = []
    for c in order:
        r0 = c * GCHUNK
        for h in range(GH):
            b = bc[r0:r0 + GCHUNK, h * GDK:(h + 1) * GDK]
            bl = b[0:1] if backward else b[GCHUNK - 1:GCHUNK]
            q = qk_ref[r0:r0 + GCHUNK, h * GDK:(h + 1) * GDK].astype(F32)
            k = qk_ref[r0:r0 + GCHUNK, GKEY + h * GDK:GKEY + (h + 1) * GDK].astype(F32)
            qe = (q * jnp.exp(b)).astype(BF16)
            ke = (k * jnp.exp(-b)).astype(BF16)
            kd = (k * jnp.exp(bl - b)).astype(BF16)
            a = jnp.where(mask, _dot_nt(qe, ke), 0.0).astype(BF16)
            items.append(dict(r0=r0, h=h, qe=qe, kd=kd, a=a, dec=jnp.exp(bl)))
    for it in items:
        v = v_ref[it["r0"]:it["r0"] + GCHUNK, it["h"] * GDV:(it["h"] + 1) * GDV]
        it["oi"] = _dot(it.pop("a"), v)
        it["u"] = _dot_tn(v, it.pop("kd"))
    return items


def _gla_kernel(qkf_ref, vf_ref, gf_ref, qkb_ref, vb_ref, gb_ref, of_ref, ob_ref, s_ref):
    @pl.when(pl.program_id(1) == 0)
    def _():
        s_ref[...] = jnp.zeros_like(s_ref)

    scans = [(_gla_local(qkf_ref, vf_ref, gf_ref, False), of_ref, 0),
             (_gla_local(qkb_ref, vb_ref, gb_ref, True), ob_ref, 1)]
    state = [[s_ref[d, h] for h in range(GH)] for d in range(2)]
    for n in range(GLA_NCH * GH):
        for items, o_ref, d in scans:
            it = items[n]
            r0, h = it["r0"], it["h"]
            s = state[d][h]
            o_ref[r0:r0 + GCHUNK, h * GDV:(h + 1) * GDV] = it["oi"] + _dot_nt(it["qe"], s.astype(BF16))
            state[d][h] = s * it["dec"] + it["u"]
    for d in range(2):
        for h in range(GH):
            s_ref[d, h] = state[d][h]


def _gla_call(qk, v, lfb):
    nxt = L // GLA_TILE

    def fwd(b, t):
        return jnp.where(t == 0, NX // GLA_TILE + b, b * nxt + t - 1)

    def bwd(b, t):
        return jnp.where(t == 0, NX // GLA_TILE + b, b * nxt + nxt - t)

    def specs(blk, d):
        return [pl.BlockSpec((GLA_TILE, 2 * GKEY), lambda b, t: (blk(b, t), 0)),
                pl.BlockSpec((GLA_TILE, GVAL), lambda b, t: (blk(b, t), 0)),
                pl.BlockSpec((GLA_TILE, GKEY), lambda b, t: (blk(b, t), d))]

    return pl.pallas_call(
        _gla_kernel,
        grid=(B, 1 + nxt),
        in_specs=specs(fwd, 0) + specs(bwd, 1),
        out_specs=[pl.BlockSpec((GLA_TILE, GVAL), lambda b, t: (fwd(b, t), 0)),
                   pl.BlockSpec((GLA_TILE, GVAL), lambda b, t: (bwd(b, t), 0))],
        out_shape=[jax.ShapeDtypeStruct((NT, GVAL), F32), jax.ShapeDtypeStruct((NT, GVAL), F32)],
        scratch_shapes=[pltpu.VMEM((2, GH, GDV, GDK), F32)],
        compiler_params=_cparams(("arbitrary", "arbitrary")),
        name="gla",
    )(qk, v, lfb, qk, v, lfb)


ATT_TQ = 1024
ATT_KC = 512
ATT_AHEAD = 2
ATT_SLACK = 64.0
ATT_HP = 2


def _flash_t(streams, lazy_max, ahead_n):
    n = len(streams[0][1])
    m = [None] * len(streams)
    acc = [None] * len(streams)
    excess = [None] * len(streams)
    ahead = [[_dot_nt(ch[j][0], q) for j in range(min(ahead_n, n))] for q, ch in streams]
    for j in range(n):
        for i, (q, ch) in enumerate(streams):
            s = ahead[i].pop(0)
            if j + ahead_n < n:
                ahead[i].append(_dot_nt(ch[j + ahead_n][0], q))
            vt = ch[j][1]
            cm = jnp.max(s, axis=0, keepdims=True)
            if j == 0:
                m[i] = cm
                acc[i] = _dot(vt, jnp.exp2(s - cm).astype(BF16))
                continue
            m_new = jnp.maximum(m[i], cm)
            if lazy_max:
                over = cm - m[i]
                excess[i] = over if excess[i] is None else jnp.maximum(excess[i], over)
                acc[i] = (acc[i] + _dot(vt, jnp.exp2(s - m[i]).astype(BF16))) * jnp.exp2(m[i] - m_new)
            else:
                acc[i] = jnp.exp2(m[i] - m_new) * acc[i] + _dot(vt, jnp.exp2(s - m_new).astype(BF16))
            m[i] = m_new
    return [a[:MV, :] / a[MV:MV + 1, :] for a in acc], excess


def _attn_kernel(q_ref, kx_ref, kc_ref, vtx_ref, vtc_ref, o_ref):
    def run(lazy_max, ahead_n):
        streams = []
        for h in range(ATT_HP):
            chunks = [(kc_ref[h], vtc_ref[h])]
            for j in range(L // ATT_KC):
                ks = slice(j * ATT_KC, (j + 1) * ATT_KC)
                chunks.append((kx_ref[h, ks, :], vtx_ref[h, :, ks]))
            streams.append((q_ref[h], chunks))
        outs, excess = _flash_t(streams, lazy_max, ahead_n)
        for h, o in enumerate(outs):
            o_ref[:, h * MV:(h + 1) * MV] = o.T.astype(BF16)
        return excess

    excess = run(True, 1)
    worst = functools.reduce(jnp.maximum, [jnp.max(e) for e in excess])

    @pl.when(worst > ATT_SLACK)
    def _():
        run(False, ATT_AHEAD)


def _attn_ctx_kernel(q_ref, kc_ref, vtc_ref, prev_ref, o_ref):
    del prev_ref
    outs, _ = _flash_t([(q_ref[h], [(kc_ref[h], vtc_ref[h])]) for h in range(MH)], False, 1)
    for h, o in enumerate(outs):
        o_ref[:, h * MV:(h + 1) * MV] = o.T.astype(BF16)


def _attn_call(qa, ka, vta, with_ctx):
    nq = L // ATT_TQ
    hp = ATT_HP
    out = pl.pallas_call(
        _attn_kernel,
        grid=(B, MH // hp, nq),
        in_specs=[pl.BlockSpec((hp, ATT_TQ, MHD), lambda b, h, i: (h, b * nq + i, 0)),
                  pl.BlockSpec((hp, L, MHD), lambda b, h, i: (h, b, 0)),
                  pl.BlockSpec((hp, LC, MHD), lambda b, h, i: (h, NX // LC + b, 0)),
                  pl.BlockSpec((hp, MVA, L), lambda b, h, i: (h, 0, b)),
                  pl.BlockSpec((hp, MVA, LC), lambda b, h, i: (h, 0, NX // LC + b))],
        out_specs=pl.BlockSpec((ATT_TQ, hp * MV), lambda b, h, i: (b * nq + i, h)),
        out_shape=jax.ShapeDtypeStruct((NT if with_ctx else NX, MH * MV), BF16),
        compiler_params=_cparams(("arbitrary", "arbitrary", "arbitrary")),
        name="attn",
    )(qa, ka, ka, vta, vta)
    if not with_ctx:
        return out
    return pl.pallas_call(
        _attn_ctx_kernel,
        grid=(B,),
        in_specs=[pl.BlockSpec((MH, LC, MHD), lambda b: (0, NX // LC + b, 0)),
                  pl.BlockSpec((MH, LC, MHD), lambda b: (0, NX // LC + b, 0)),
                  pl.BlockSpec((MH, MVA, LC), lambda b: (0, 0, NX // LC + b)),
                  pl.BlockSpec(memory_space=pl.ANY)],
        out_specs=pl.BlockSpec((LC, MH * MV), lambda b: (NX // LC + b, 0)),
        out_shape=jax.ShapeDtypeStruct((NT, MH * MV), BF16),
        input_output_aliases={3: 0},
        compiler_params=_cparams(("arbitrary",)),
        name="attn_ctx",
    )(qa, ka, vta, out)


MERGE_TM = 512


def _merge_kernel(x_ref, of_ref, ob_ref, r_ref, gn_ref, zb_ref, u_ref, up_ref, un_ref, cw_ref,
                  c_ref, gt_ref, wa_ref, wb_ref, wc_ref, wo_ref, g5_ref, lw_ref, lb_ref, o_ref):
    tm = MERGE_TM
    c = _dot(c_ref[...], wc_ref[...])
    o = of_ref[...] + ob_ref[...]
    parts = []
    for h in range(GH):
        oh = o[:, h * GDV:(h + 1) * GDV]
        parts.append(oh * lax.rsqrt(jnp.mean(oh * oh, axis=-1, keepdims=True) + EPS))
    a_in = (jnp.concatenate(parts, axis=1) * gn_ref[...] * r_ref[...].astype(F32)).astype(BF16)
    u = u_ref[...].astype(F32)
    prev = up_ref[7:8, :].astype(F32)
    nxt = un_ref[0:1, :].astype(F32)
    loc = lax.broadcasted_iota(jnp.int32, (tm, 1), 0)
    row = loc + pl.program_id(0) * tm
    seq = jnp.where(row >= NX, LC, L)
    pos = row & (seq - 1)
    ul = jnp.where(loc == 0, prev, pltpu.roll(u, 1, 0))
    ul = jnp.where(pos == 0, 0.0, ul)
    ur = jnp.where(loc == tm - 1, nxt, pltpu.roll(u, tm - 1, 0))
    ur = jnp.where(pos == seq - 1, 0.0, ur)
    cv = ul * cw_ref[0:1, :] + u * cw_ref[1:2, :] + ur * cw_ref[2:3, :]
    b_in = (zb_ref[...].astype(F32) * cv).astype(BF16)
    subs = _subtiles(x_ref)
    ab = [(_dot(a_in[rs, :], wa_ref[...]), _dot(b_in[rs, :], wb_ref[...])) for rs in subs]
    for rs, (a, bb) in zip(subs, ab):
        mix = (gt_ref[rs, :D].astype(F32) * a + gt_ref[rs, D:2 * D].astype(F32) * bb
               + gt_ref[rs, 2 * D:].astype(F32) * c[rs, :])
        mx = _dot(mix.astype(BF16), wo_ref[...])
        z = ALPHA * x_ref[rs, :] + g5_ref[...] * mx
        o_ref[rs, :] = _layer_norm(z, lw_ref[...], lb_ref[...])


def _merge_call(x1, o_f, o_b, r, gnorm, zb, u, conv_w, c_pre, gates, wa, wb, wc, wo, g5, lw, lb, rows):
    tm = MERGE_TM
    row = lambda w: pl.BlockSpec((tm, w), lambda i: (i, 0))
    last8 = rows // 8 - 1
    return pl.pallas_call(
        _merge_kernel,
        grid=(rows // tm,),
        in_specs=[row(D), row(GVAL), row(GVAL), row(GVAL), _const((1, GVAL)),
                  row(CW), row(CW),
                  pl.BlockSpec((8, CW), lambda i: (jnp.maximum(i * (tm // 8) - 1, 0), 0)),
                  pl.BlockSpec((8, CW), lambda i: (jnp.minimum((i + 1) * (tm // 8), last8), 0)),
                  _const((3, CW)),
                  row(MH * MV), row(3 * D),
                  _const((GVAL, D)), _const((CW, D)), _const((MH * MV, D)), _const((D, D)),
                  _mod_spec(tm), _const((1, D)), _const((1, D))],
        out_specs=row(D),
        out_shape=jax.ShapeDtypeStruct((rows, D), F32),
        compiler_params=_cparams(("arbitrary",)),
        name="merge",
    )(x1, o_f, o_b, r, gnorm, zb, u, u, u, conv_w, c_pre, gates, wa, wb, wc, wo, g5, lw, lb)


def _rope_tables():
    t = jnp.arange(L, dtype=jnp.int32)
    quarter = MROPE // 4
    inv_freq = THETA ** (-jnp.arange(quarter, dtype=F32) / quarter)

    def cs(pos):
        ang = pos.astype(F32)[:, None] * inv_freq
        return jnp.cos(ang), jnp.sin(ang)

    cr, sr = cs(t // GRID_W)
    cc, sn = cs(t % GRID_W)
    tab = jnp.concatenate([cr, cr, cc, cc, -sr, sr, -sn, sn], axis=1)
    ident = jnp.concatenate([jnp.ones((NC, MROPE), F32), jnp.zeros((NC, MROPE), F32)], axis=1)
    return jnp.concatenate([jnp.tile(tab, (B, 1)), ident], axis=0)


def _swap_rope_cols(w):
    q = MROPE // 4
    return jnp.concatenate([w[..., q:2 * q], w[..., :q], w[..., 3 * q:], w[..., 2 * q:3 * q]], axis=-1)


def _layer_weights(l, w_in, gla_decay_w, gla_decay_b, mla_w_uq, mla_w_ukv):
    wi = w_in[l]
    o = 0
    cols = {}
    for name, n in (("q", GKEY), ("k", GKEY), ("v", GVAL), ("r", GVAL), ("gf", GRANK), ("gb", GRANK),
                    ("cb", CW), ("cc", CW), ("cx", CW), ("cq", MQR), ("ckv", MKVR), ("kr", MROPE),
                    ("zg", 3 * D)):
        cols[name] = wi[:, o:o + n]
        o += n
    pad = jnp.zeros((D, 128 - 2 * GRANK), F32)
    w1 = jnp.concatenate([cols["gf"], cols["gb"], pad, cols["q"], cols["k"], cols["v"], cols["r"]], axis=1)
    w2 = jnp.concatenate([cols["cb"], cols["cc"], cols["cx"]], axis=1)
    w4 = jnp.concatenate([cols["cq"], cols["ckv"], cols["kr"], _swap_rope_cols(cols["kr"])], axis=1)
    dw = jnp.zeros((128, 2 * GKEY), F32)
    dw = dw.at[:GRANK, :GKEY].set(gla_decay_w[l, 0]).at[GRANK:2 * GRANK, GKEY:].set(gla_decay_w[l, 1])
    db = jnp.concatenate([gla_decay_b[l, 0], gla_decay_b[l, 1]])[None, :]
    wq = mla_w_uq[l].reshape(MQR, MH, MNOPE + MROPE)
    wq = jnp.concatenate([wq, _swap_rope_cols(wq[..., MNOPE:])], axis=-1).reshape(MQR, MH * MHD)
    wkv = mla_w_ukv[l].reshape(MKVR, MH, MNOPE + MV)
    wk = wkv[..., :MNOPE].reshape(MKVR, MH * MNOPE)
    wvt = wkv[..., MNOPE:].reshape(MKVR, MH * MV).T
    bf = lambda a: a.astype(BF16)
    return dict(w1=bf(w1), w2=bf(w2), w3=bf(cols["zg"]), w4=bf(w4), dw=bf(dw), db=db, wq=bf(wq),
                wk=bf(wk), wvt=bf(wvt))


def kernel(x, c, ctx, c_ctx, ada_w, ada_b, ln_w, ln_b, ffn_w13, ffn_w2, w_in, b_gate, gla_decay_w,
           gla_decay_b, gla_norm, gla_proj, conv_w, conv_proj, mla_q_norm, mla_w_uq, mla_kv_norm,
           mla_w_ukv, mla_proj, w_out):
    assert x.shape == (B, L, D) and ctx.shape == (B, LC, D)
    c_all = jnp.concatenate([c, c_ctx[None, :], jnp.zeros((MODROWS - B - 1, D), F32)], axis=0)
    mods = _ada_call(c_all, ada_w, ada_b).reshape(DEPTH, MODROWS, NMOD, 1, D)
    cs = _rope_tables()
    h = (x.reshape(NX, D), ctx.reshape(NC, D))
    vec = lambda a: a.reshape(1, -1)
    for l in range(DEPTH):
        last = l == DEPTH - 1
        rows = NX if last else NT
        m = [mods[l, :, k] for k in range(NMOD)]
        lw = _layer_weights(l, w_in, gla_decay_w, gla_decay_b, mla_w_uq, mla_w_ukv)
        x1, hm = _ffn_call(h, NT, m[0:3], ffn_w13[l, 0].astype(BF16), ffn_w2[l, 0].astype(BF16),
                           vec(ln_w[l, 0]), vec(ln_b[l, 0]), mod_next=(m[3], m[4]))
        qk, v, r, lfb = _p1_call(hm, lw["w1"], lw["dw"], lw["db"])
        zb, u = _p2_call(hm, lw["w2"], rows)
        gates = _p3_call(hm, lw["w3"], vec(b_gate[l]), rows)
        qa, ka, vta = _p4_call(hm, lw["w4"], cs, vec(mla_q_norm[l]), lw["wq"], vec(mla_kv_norm[l]),
                               lw["wk"], lw["wvt"])
        o_f, o_b = _gla_call(qk, v, lfb)
        c_pre = _attn_call(qa, ka, vta, with_ctx=not last)
        x2 = _merge_call(x1, o_f, o_b, r, vec(gla_norm[l]), zb, u, conv_w[l], c_pre, gates,
                         gla_proj[l].astype(BF16), conv_proj[l].astype(BF16), mla_proj[l].astype(BF16),
                         w_out[l].astype(BF16), m[5], vec(ln_w[l, 1]), vec(ln_b[l, 1]), rows)
        h = _ffn_call(x2, rows, m[6:9], ffn_w13[l, 1].astype(BF16), ffn_w2[l, 1].astype(BF16),
                      vec(ln_w[l, 2]), vec(ln_b[l, 2]))
    return h.reshape(B, L, D)
```

```python
import functools

import jax
import jax.numpy as jnp
from jax import lax
from jax.experimental import pallas as pl
from jax.experimental.pallas import tpu as pltpu

F32 = jnp.float32
BF16 = jnp.bfloat16

D = 1024
B = 8
L = 4096
DEPTH = 4
LC = 256
GRID_W = 64
DFF = 2816
GH, GDK, GDV = 4, 128, 256
GKEY, GVAL = GH * GDK, GH * GDV
GRANK = 16
GTEMP = 16.0
GCHUNK = 64
CW = 1024
MH, MQR, MKVR, MNOPE, MROPE, MV = 8, 512, 256, 128, 64, 128
THETA = 10000.0
NMOD = 9
EPS = 1e-6
ALPHA = (2.0 * DEPTH) ** 0.25

NX = B * L
NC = B * LC
NT = NX + NC
MODROWS = 16
MHD = 256
MVA = MV + 16

VMEM_LIMIT = 56 * 1024 * 1024


def _cparams(sem):
    return pltpu.CompilerParams(dimension_semantics=sem, vmem_limit_bytes=VMEM_LIMIT)


def _const(shape):
    n = len(shape)
    return pl.BlockSpec(shape, lambda *_: (0,) * n, pipeline_mode=pl.Buffered(1))


def _mod_spec(tm):
    per = L // tm
    return pl.BlockSpec((None, 1, D), lambda i: (jnp.minimum(i // per, B), 0, 0))


def _silu(x):
    return x * jax.nn.sigmoid(x)


def _layer_norm(z, w, b):
    mu = jnp.mean(z, axis=-1, keepdims=True)
    zc = z - mu
    var = jnp.mean(zc * zc, axis=-1, keepdims=True)
    return zc * lax.rsqrt(var + EPS) * w + b


def _rms(x, w):
    return x * lax.rsqrt(jnp.mean(x * x, axis=-1, keepdims=True) + EPS) * w


def _dot(a, b):
    return jnp.dot(a, b, preferred_element_type=F32)


def _dot_nt(a, b):
    return lax.dot_general(a, b, (((1,), (1,)), ((), ())), preferred_element_type=F32)


def _dot_tn(a, b):
    return lax.dot_general(a, b, (((0,), (0,)), ((), ())), preferred_element_type=F32)


def _ada_kernel(c_ref, w_ref, b_ref, o_ref):
    sc = _silu(c_ref[...]).astype(BF16)
    o_ref[...] = _dot(sc, w_ref[...].astype(BF16)) + b_ref[...]


def _ada_call(c_all, ada_w, ada_b):
    tn = 1152
    return pl.pallas_call(
        _ada_kernel,
        grid=(DEPTH, NMOD * D // tn),
        in_specs=[
            pl.BlockSpec((MODROWS, D), lambda l, j: (0, 0)),
            pl.BlockSpec((None, D, tn), lambda l, j: (l, 0, j)),
            pl.BlockSpec((None, 1, tn), lambda l, j: (l, 0, j)),
        ],
        out_specs=pl.BlockSpec((None, MODROWS, tn), lambda l, j: (l, 0, j)),
        out_shape=jax.ShapeDtypeStruct((DEPTH, MODROWS, NMOD * D), F32),
        compiler_params=_cparams(("arbitrary", "arbitrary")),
        name="ada",
    )(c_all, ada_w, ada_b.reshape(DEPTH, 1, NMOD * D))


SUB_ROWS = 256
PROJ_TM = 1024

def _ffn_kernel(x_ref, *rest, emit_mod, split_tiles):
    if split_tiles:
        xc_ref, *rest = rest
        is_latent = pl.program_id(0) < split_tiles
    sh_ref, sc_ref, g_ref, w13_ref, w2_ref, lw_ref, lb_ref, *rest = rest
    for r0 in range(0, x_ref.shape[0], SUB_ROWS):
        rs = slice(r0, r0 + SUB_ROWS)
        x = x_ref[rs, :]
        if split_tiles:
            x = jnp.where(is_latent, x, xc_ref[rs, :])
        hm = (x * (1.0 + sc_ref[...]) + sh_ref[...]).astype(BF16)
        a = _dot(hm, w13_ref[:, :DFF])
        g = _dot(hm, w13_ref[:, DFF:])
        act = (_silu(a) * g).astype(BF16)
        y = _dot(act, w2_ref[...])
        z = ALPHA * x + (0.5 * g_ref[...]) * y
        out = _layer_norm(z, lw_ref[...], lb_ref[...])
        if emit_mod:
            sh2_ref, sc2_ref, o_ref, om_ref = rest
            o_ref[rs, :] = out
            om_ref[rs, :] = (out * (1.0 + sc2_ref[...]) + sh2_ref[...]).astype(BF16)
        else:
            (o_ref,) = rest
            o_ref[rs, :] = out


def _ffn_call(h, rows, mod3, w13, w2, lw, lb, mod_next=None):
    tm = 1024
    emit = mod_next is not None
    row = pl.BlockSpec((tm, D), lambda i: (i, 0))
    if isinstance(h, tuple):
        nxt = NX // tm
        tokens = list(h)
        token_specs = [pl.BlockSpec((tm, D), lambda i: (jnp.minimum(i, nxt - 1), 0)),
                       pl.BlockSpec((tm, D), lambda i: (jnp.maximum(i - nxt, 0), 0))]
    else:
        nxt = 0
        tokens = [h]
        token_specs = [row]
    in_specs = token_specs + [_mod_spec(tm), _mod_spec(tm), _mod_spec(tm),
                              _const((D, 2 * DFF)), _const((DFF, D)), _const((1, D)), _const((1, D))]
    args = [*tokens, *mod3, w13, w2, lw, lb]
    out_shape = [jax.ShapeDtypeStruct((rows, D), F32)]
    out_specs = [row]
    if emit:
        in_specs += [_mod_spec(tm), _mod_spec(tm)]
        args += list(mod_next)
        out_shape.append(jax.ShapeDtypeStruct((rows, D), BF16))
        out_specs.append(row)
    res = pl.pallas_call(
        functools.partial(_ffn_kernel, emit_mod=emit, split_tiles=nxt),
        grid=(rows // tm,),
        in_specs=in_specs,
        out_specs=out_specs,
        out_shape=out_shape,
        compiler_params=_cparams(("arbitrary",)),
        name="ffn",
    )(*args)
    return res if emit else res[0]


def _subtiles(ref, rows=SUB_ROWS):
    return [slice(r0, r0 + rows) for r0 in range(0, ref.shape[0], rows)]


def _p1_kernel(hm_ref, w_ref, dw_ref, db_ref, qk_ref, v_ref, r_ref, lfb_ref):
    n0 = 128 + 2 * GKEY
    for rs in _subtiles(hm_ref):
        hm = hm_ref[rs, :]
        z0 = _dot(hm, w_ref[:, :n0])
        xg = _dot(z0[:, :128].astype(BF16), dw_ref[...]) + db_ref[...]
        lfb_ref[rs, :] = (jnp.minimum(xg, 0.0) - jnp.log(1.0 + jnp.exp(-jnp.abs(xg)))) * (1.0 / GTEMP)
        qk_ref[rs, :GKEY] = (z0[:, 128:128 + GKEY] * (GDK ** -0.5)).astype(BF16)
        qk_ref[rs, GKEY:] = z0[:, 128 + GKEY:].astype(BF16)
        z1 = _dot(hm, w_ref[:, n0:])
        v_ref[rs, :] = z1[:, :GVAL].astype(BF16)
        r_ref[rs, :] = _silu(z1[:, GVAL:]).astype(BF16)


def _p1_call(hm, w, dw, db):
    tm = PROJ_TM
    return pl.pallas_call(
        _p1_kernel,
        grid=(NT // tm,),
        in_specs=[pl.BlockSpec((tm, D), lambda i: (i, 0)),
                  _const((D, 128 + 2 * GKEY + 2 * GVAL)),
                  _const((128, 2 * GKEY)), _const((1, 2 * GKEY))],
        out_specs=[pl.BlockSpec((tm, 2 * GKEY), lambda i: (i, 0)),
                   pl.BlockSpec((tm, GVAL), lambda i: (i, 0)),
                   pl.BlockSpec((tm, GVAL), lambda i: (i, 0)),
                   pl.BlockSpec((tm, 2 * GKEY), lambda i: (i, 0))],
        out_shape=[jax.ShapeDtypeStruct((NT, 2 * GKEY), BF16),
                   jax.ShapeDtypeStruct((NT, GVAL), BF16),
                   jax.ShapeDtypeStruct((NT, GVAL), BF16),
                   jax.ShapeDtypeStruct((NT, 2 * GKEY), F32)],
        compiler_params=_cparams(("arbitrary",)),
        name="proj_gla",
    )(hm, w, dw, db)


def _p2_kernel(hm_ref, w_ref, zb_ref, u_ref):
    for rs in _subtiles(hm_ref):
        z = _dot(hm_ref[rs, :], w_ref[...])
        zb_ref[rs, :] = z[:, :CW].astype(BF16)
        u_ref[rs, :] = (z[:, CW:2 * CW] * z[:, 2 * CW:]).astype(BF16)


def _p2_call(hm, w, rows):
    tm = PROJ_TM
    return pl.pallas_call(
        _p2_kernel,
        grid=(rows // tm,),
        in_specs=[pl.BlockSpec((tm, D), lambda i: (i, 0)), _const((D, 3 * CW))],
        out_specs=[pl.BlockSpec((tm, CW), lambda i: (i, 0)),
                   pl.BlockSpec((tm, CW), lambda i: (i, 0))],
        out_shape=[jax.ShapeDtypeStruct((rows, CW), BF16),
                   jax.ShapeDtypeStruct((rows, CW), BF16)],
        compiler_params=_cparams(("arbitrary",)),
        name="proj_conv",
    )(hm, w)


def _p3_kernel(hm_ref, w_ref, b_ref, g_ref):
    for rs in _subtiles(hm_ref):
        z = _dot(hm_ref[rs, :], w_ref[...]) + b_ref[...]
        g_ref[rs, :] = jax.nn.sigmoid(z).astype(BF16)


def _p3_call(hm, w, b_gate, rows):
    tm = PROJ_TM
    return pl.pallas_call(
        _p3_kernel,
        grid=(rows // tm,),
        in_specs=[pl.BlockSpec((tm, D), lambda i: (i, 0)), _const((D, 3 * D)), _const((1, 3 * D))],
        out_specs=pl.BlockSpec((tm, 3 * D), lambda i: (i, 0)),
        out_shape=jax.ShapeDtypeStruct((rows, 3 * D), BF16),
        compiler_params=_cparams(("arbitrary",)),
        name="proj_gate",
    )(hm, w, b_gate)


def _rope_fold(t, cs):
    rr = t * cs
    return rr + pltpu.roll(rr, MROPE, 1)


Q_SCALE = (MNOPE + MROPE) ** -0.5 * 1.4426950408889634
MLA_SUB = 512


def _p4_kernel(hm_ref, w_ref, cs_ref, qn_ref, wuq_ref, kvn_ref, wuk_ref, wuvt_ref, q_ref, k_ref, vt_ref):
    for rs in _subtiles(hm_ref, MLA_SUB):
        z = _dot(hm_ref[rs, :], w_ref[...])
        cs = cs_ref[rs, :]
        cqn = _rms(z[:, :MQR], qn_ref[...]).astype(BF16)
        kvn = _rms(z[:, MQR:MQR + MKVR], kvn_ref[...]).astype(BF16)
        qz = _dot(cqn, wuq_ref[...]) * Q_SCALE
        kz = _dot(kvn, wuk_ref[...])
        vt = _dot_nt(wuvt_ref[...], kvn)
        for h in range(MH):
            base = h * MHD
            rot = _rope_fold(qz[:, base + MNOPE:base + MHD], cs)
            q_ref[h, rs, :] = jnp.concatenate([qz[:, base:base + MNOPE], rot], axis=1).astype(BF16)
        krot = _rope_fold(z[:, MQR + MKVR:], cs)
        lane = lax.broadcasted_iota(jnp.int32, krot.shape, 1)
        krot = jnp.where(lane < MROPE, krot, 0.0)
        for h in range(MH):
            k_ref[h, rs, :] = jnp.concatenate([kz[:, h * MNOPE:(h + 1) * MNOPE], krot], axis=1).astype(BF16)
            vt_ref[h, :MV, rs] = vt[h * MV:(h + 1) * MV, :].astype(BF16)
            vt_ref[h, MV:, rs] = jnp.ones((MVA - MV, MLA_SUB), BF16)


def _p4_call(hm, w, cs, qn, wuq, kvn, wuk, wuvt):
    tm = PROJ_TM
    return pl.pallas_call(
        _p4_kernel,
        grid=(NT // tm,),
        in_specs=[pl.BlockSpec((tm, D), lambda i: (i, 0)), _const((D, MQR + MKVR + 2 * MROPE)),
                  pl.BlockSpec((tm, 2 * MROPE), lambda i: (i, 0)),
                  _const((1, MQR)), _const((MQR, MH * MHD)),
                  _const((1, MKVR)), _const((MKVR, MH * MNOPE)), _const((MH * MV, MKVR))],
        out_specs=[pl.BlockSpec((MH, tm, MHD), lambda i: (0, i, 0)),
                   pl.BlockSpec((MH, tm, MHD), lambda i: (0, i, 0)),
                   pl.BlockSpec((MH, MVA, tm), lambda i: (0, 0, i))],
        out_shape=[jax.ShapeDtypeStruct((MH, NT, MHD), BF16),
                   jax.ShapeDtypeStruct((MH, NT, MHD), BF16),
                   jax.ShapeDtypeStruct((MH, MVA, NT), BF16)],
        compiler_params=_cparams(("arbitrary",)),
        name="proj_mla",
    )(hm, w, cs, qn, wuq, kvn, wuk, wuvt)


GLA_TILE = 256
GLA_NCH = GLA_TILE // GCHUNK


def _gla_local(qk_ref, v_ref, g_ref, backward):
    g = g_ref[...]
    ri = lax.broadcasted_iota(jnp.int32, (GLA_TILE, GLA_TILE), 0)
    ci = lax.broadcasted_iota(jnp.int32, (GLA_TILE, GLA_TILE), 1)
    same = (ri // GCHUNK) == (ci // GCHUNK)
    tri = jnp.where(same & ((ci >= ri) if backward else (ci <= ri)), 1.0, 0.0).astype(BF16)
    g_hi = g.astype(BF16)
    g_lo = (g - g_hi.astype(F32)).astype(BF16)
    bc = _dot(tri, g_hi) + _dot(tri, g_lo)
    r64 = lax.broadcasted_iota(jnp.int32, (GCHUNK, GCHUNK), 0)
    c64 = lax.broadcasted_iota(jnp.int32, (GCHUNK, GCHUNK), 1)
    mask = (c64 >= r64) if backward else (c64 <= r64)
    order = range(GLA_NCH - 1, -1, -1) if backward else range(GLA_NCH)
    items = []
    for c in order:
        r0 = c * GCHUNK
        for h in range(GH):
            b = bc[r0:r0 + GCHUNK, h * GDK:(h + 1) * GDK]
            bl = b[0:1] if backward else b[GCHUNK - 1:GCHUNK]
            q = qk_ref[r0:r0 + GCHUNK, h * GDK:(h + 1) * GDK].astype(F32)
            k = qk_ref[r0:r0 + GCHUNK, GKEY + h * GDK:GKEY + (h + 1) * GDK].astype(F32)
            qe = (q * jnp.exp(b)).astype(BF16)
            ke = (k * jnp.exp(-b)).astype(BF16)
            kd = (k * jnp.exp(bl - b)).astype(BF16)
            a = jnp.where(mask, _dot_nt(qe, ke), 0.0).astype(BF16)
            items.append(dict(r0=r0, h=h, qe=qe, kd=kd, a=a, dec=jnp.exp(bl)))
    for it in items:
        v = v_ref[it["r0"]:it["r0"] + GCHUNK, it["h"] * GDV:(it["h"] + 1) * GDV]
        it["oi"] = _dot(it.pop("a"), v)
        it["u"] = _dot_tn(v, it.pop("kd"))
    return items


def _gla_kernel(qkf_ref, vf_ref, gf_ref, qkb_ref, vb_ref, gb_ref, of_ref, ob_ref, s_ref):
    @pl.when(pl.program_id(1) == 0)
    def _():
        s_ref[...] = jnp.zeros_like(s_ref)

    scans = [(_gla_local(qkf_ref, vf_ref, gf_ref, False), of_ref, 0),
             (_gla_local(qkb_ref, vb_ref, gb_ref, True), ob_ref, 1)]
    state = [[s_ref[d, h] for h in range(GH)] for d in range(2)]
    for n in range(GLA_NCH * GH):
        for items, o_ref, d in scans:
            it = items[n]
            r0, h = it["r0"], it["h"]
            s = state[d][h]
            o_ref[r0:r0 + GCHUNK, h * GDV:(h + 1) * GDV] = it["oi"] + _dot_nt(it["qe"], s.astype(BF16))
            state[d][h] = s * it["dec"] + it["u"]
    for d in range(2):
        for h in range(GH):
            s_ref[d, h] = state[d][h]


def _gla_call(qk, v, lfb):
    nxt = L // GLA_TILE

    def fwd(b, t):
        return jnp.where(t == 0, NX // GLA_TILE + b, b * nxt + t - 1)

    def bwd(b, t):
        return jnp.where(t == 0, NX // GLA_TILE + b, b * nxt + nxt - t)

    def specs(blk, d):
        return [pl.BlockSpec((GLA_TILE, 2 * GKEY), lambda b, t: (blk(b, t), 0)),
                pl.BlockSpec((GLA_TILE, GVAL), lambda b, t: (blk(b, t), 0)),
                pl.BlockSpec((GLA_TILE, GKEY), lambda b, t: (blk(b, t), d))]

    return pl.pallas_call(
        _gla_kernel,
        grid=(B, 1 + nxt),
        in_specs=specs(fwd, 0) + specs(bwd, 1),
        out_specs=[pl.BlockSpec((GLA_TILE, GVAL), lambda b, t: (fwd(b, t), 0)),
                   pl.BlockSpec((GLA_TILE, GVAL), lambda b, t: (bwd(b, t), 0))],
        out_shape=[jax.ShapeDtypeStruct((NT, GVAL), F32), jax.ShapeDtypeStruct((NT, GVAL), F32)],
        scratch_shapes=[pltpu.VMEM((2, GH, GDV, GDK), F32)],
        compiler_params=_cparams(("arbitrary", "arbitrary")),
        name="gla",
    )(qk, v, lfb, qk, v, lfb)


ATT_TQ = 1024
ATT_KC = 1024
ATT_AHEAD = 2
ATT_SLACK = 64.0
ATT_HP = 2


def _flash_t(streams, lazy_max, ahead_n):
    n = len(streams[0][1])
    m = [None] * len(streams)
    acc = [None] * len(streams)
    excess = [None] * len(streams)
    ahead = [[_dot_nt(ch[j][0], q) for j in range(min(ahead_n, n))] for q, ch in streams]
    for j in range(n):
        for i, (q, ch) in enumerate(streams):
            s = ahead[i].pop(0)
            if j + ahead_n < n:
                ahead[i].append(_dot_nt(ch[j + ahead_n][0], q))
            vt = ch[j][1]
            cm = jnp.max(s, axis=0, keepdims=True)
            if j == 0:
                m[i] = cm
                acc[i] = _dot(vt, jnp.exp2(s - cm).astype(BF16))
                continue
            m_new = jnp.maximum(m[i], cm)
            if lazy_max:
                over = cm - m[i]
                excess[i] = over if excess[i] is None else jnp.maximum(excess[i], over)
                acc[i] = (acc[i] + _dot(vt, jnp.exp2(s - m[i]).astype(BF16))) * jnp.exp2(m[i] - m_new)
            else:
                acc[i] = jnp.exp2(m[i] - m_new) * acc[i] + _dot(vt, jnp.exp2(s - m_new).astype(BF16))
            m[i] = m_new
    return [a[:MV, :] / a[MV:MV + 1, :] for a in acc], excess


def _attn_kernel(q_ref, kx_ref, kc_ref, vtx_ref, vtc_ref, o_ref):
    def run(lazy_max, ahead_n):
        streams = []
        for h in range(ATT_HP):
            chunks = [(kc_ref[h], vtc_ref[h])]
            for j in range(L // ATT_KC):
                ks = slice(j * ATT_KC, (j + 1) * ATT_KC)
                chunks.append((kx_ref[h, ks, :], vtx_ref[h, :, ks]))
            streams.append((q_ref[h], chunks))
        outs, excess = _flash_t(streams, lazy_max, ahead_n)
        for h, o in enumerate(outs):
            o_ref[:, h * MV:(h + 1) * MV] = o.T.astype(BF16)
        return excess

    excess = run(True, 1)
    worst = functools.reduce(jnp.maximum, [jnp.max(e) for e in excess])

    @pl.when(worst > ATT_SLACK)
    def _():
        run(False, ATT_AHEAD)


def _attn_ctx_kernel(q_ref, kc_ref, vtc_ref, prev_ref, o_ref):
    del prev_ref
    outs, _ = _flash_t([(q_ref[h], [(kc_ref[h], vtc_ref[h])]) for h in range(MH)], False, 1)
    for h, o in enumerate(outs):
        o_ref[:, h * MV:(h + 1) * MV] = o.T.astype(BF16)


def _attn_call(qa, ka, vta, with_ctx):
    nq = L // ATT_TQ
    hp = ATT_HP
    out = pl.pallas_call(
        _attn_kernel,
        grid=(B, MH // hp, nq),
        in_specs=[pl.BlockSpec((hp, ATT_TQ, MHD), lambda b, h, i: (h, b * nq + i, 0)),
                  pl.BlockSpec((hp, L, MHD), lambda b, h, i: (h, b, 0)),
                  pl.BlockSpec((hp, LC, MHD), lambda b, h, i: (h, NX // LC + b, 0)),
                  pl.BlockSpec((hp, MVA, L), lambda b, h, i: (h, 0, b)),
                  pl.BlockSpec((hp, MVA, LC), lambda b, h, i: (h, 0, NX // LC + b))],
        out_specs=pl.BlockSpec((ATT_TQ, hp * MV), lambda b, h, i: (b * nq + i, h)),
        out_shape=jax.ShapeDtypeStruct((NT if with_ctx else NX, MH * MV), BF16),
        compiler_params=_cparams(("arbitrary", "arbitrary", "arbitrary")),
        name="attn",
    )(qa, ka, ka, vta, vta)
    if not with_ctx:
        return out
    return pl.pallas_call(
        _attn_ctx_kernel,
        grid=(B,),
        in_specs=[pl.BlockSpec((MH, LC, MHD), lambda b: (0, NX // LC + b, 0)),
                  pl.BlockSpec((MH, LC, MHD), lambda b: (0, NX // LC + b, 0)),
                  pl.BlockSpec((MH, MVA, LC), lambda b: (0, 0, NX // LC + b)),
                  pl.BlockSpec(memory_space=pl.ANY)],
        out_specs=pl.BlockSpec((LC, MH * MV), lambda b: (NX // LC + b, 0)),
        out_shape=jax.ShapeDtypeStruct((NT, MH * MV), BF16),
        input_output_aliases={3: 0},
        compiler_params=_cparams(("arbitrary",)),
        name="attn_ctx",
    )(qa, ka, vta, out)


MERGE_TM = 512


def _merge_kernel(x_ref, of_ref, ob_ref, r_ref, gn_ref, zb_ref, u_ref, up_ref, un_ref, cw_ref,
                  c_ref, gt_ref, wa_ref, wb_ref, wc_ref, wo_ref, g5_ref, lw_ref, lb_ref, o_ref):
    tm = MERGE_TM
    c = _dot(c_ref[...], wc_ref[...])
    o = of_ref[...] + ob_ref[...]
    parts = []
    for h in range(GH):
        oh = o[:, h * GDV:(h + 1) * GDV]
        parts.append(oh * lax.rsqrt(jnp.mean(oh * oh, axis=-1, keepdims=True) + EPS))
    a_in = (jnp.concatenate(parts, axis=1) * gn_ref[...] * r_ref[...].astype(F32)).astype(BF16)
    u = u_ref[...].astype(F32)
    prev = up_ref[7:8, :].astype(F32)
    nxt = un_ref[0:1, :].astype(F32)
    loc = lax.broadcasted_iota(jnp.int32, (tm, 1), 0)
    row = loc + pl.program_id(0) * tm
    seq = jnp.where(row >= NX, LC, L)
    pos = row & (seq - 1)
    ul = jnp.where(loc == 0, prev, pltpu.roll(u, 1, 0))
    ul = jnp.where(pos == 0, 0.0, ul)
    ur = jnp.where(loc == tm - 1, nxt, pltpu.roll(u, tm - 1, 0))
    ur = jnp.where(pos == seq - 1, 0.0, ur)
    cv = ul * cw_ref[0:1, :] + u * cw_ref[1:2, :] + ur * cw_ref[2:3, :]
    b_in = (zb_ref[...].astype(F32) * cv).astype(BF16)
    subs = _subtiles(x_ref)
    ab = [(_dot(a_in[rs, :], wa_ref[...]), _dot(b_in[rs, :], wb_ref[...])) for rs in subs]
    for rs, (a, bb) in zip(subs, ab):
        mix = (gt_ref[rs, :D].astype(F32) * a + gt_ref[rs, D:2 * D].astype(F32) * bb
               + gt_ref[rs, 2 * D:].astype(F32) * c[rs, :])
        mx = _dot(mix.astype(BF16), wo_ref[...])
        z = ALPHA * x_ref[rs, :] + g5_ref[...] * mx
        o_ref[rs, :] = _layer_norm(z, lw_ref[...], lb_ref[...])


def _merge_call(x1, o_f, o_b, r, gnorm, zb, u, conv_w, c_pre, gates, wa, wb, wc, wo, g5, lw, lb, rows):
    tm = MERGE_TM
    row = lambda w: pl.BlockSpec((tm, w), lambda i: (i, 0))
    last8 = rows // 8 - 1
    return pl.pallas_call(
        _merge_kernel,
        grid=(rows // tm,),
        in_specs=[row(D), row(GVAL), row(GVAL), row(GVAL), _const((1, GVAL)),
                  row(CW), row(CW),
                  pl.BlockSpec((8, CW), lambda i: (jnp.maximum(i * (tm // 8) - 1, 0), 0)),
                  pl.BlockSpec((8, CW), lambda i: (jnp.minimum((i + 1) * (tm // 8), last8), 0)),
                  _const((3, CW)),
                  row(MH * MV), row(3 * D),
                  _const((GVAL, D)), _const((CW, D)), _const((MH * MV, D)), _const((D, D)),
                  _mod_spec(tm), _const((1, D)), _const((1, D))],
        out_specs=row(D),
        out_shape=jax.ShapeDtypeStruct((rows, D), F32),
        compiler_params=_cparams(("arbitrary",)),
        name="merge",
    )(x1, o_f, o_b, r, gnorm, zb, u, u, u, conv_w, c_pre, gates, wa, wb, wc, wo, g5, lw, lb)


def _rope_tables():
    t = jnp.arange(L, dtype=jnp.int32)
    quarter = MROPE // 4
    inv_freq = THETA ** (-jnp.arange(quarter, dtype=F32) / quarter)

    def cs(pos):
        ang = pos.astype(F32)[:, None] * inv_freq
        return jnp.cos(ang), jnp.sin(ang)

    cr, sr = cs(t // GRID_W)
    cc, sn = cs(t % GRID_W)
    tab = jnp.concatenate([cr, cr, cc, cc, -sr, sr, -sn, sn], axis=1)
    ident = jnp.concatenate([jnp.ones((NC, MROPE), F32), jnp.zeros((NC, MROPE), F32)], axis=1)
    return jnp.concatenate([jnp.tile(tab, (B, 1)), ident], axis=0)


def _swap_rope_cols(w):
    q = MROPE // 4
    return jnp.concatenate([w[..., q:2 * q], w[..., :q], w[..., 3 * q:], w[..., 2 * q:3 * q]], axis=-1)


def _layer_weights(l, w_in, gla_decay_w, gla_decay_b, mla_w_uq, mla_w_ukv):
    wi = w_in[l]
    o = 0
    cols = {}
    for name, n in (("q", GKEY), ("k", GKEY), ("v", GVAL), ("r", GVAL), ("gf", GRANK), ("gb", GRANK),
                    ("cb", CW), ("cc", CW), ("cx", CW), ("cq", MQR), ("ckv", MKVR), ("kr", MROPE),
                    ("zg", 3 * D)):
        cols[name] = wi[:, o:o + n]
        o += n
    pad = jnp.zeros((D, 128 - 2 * GRANK), F32)
    w1 = jnp.concatenate([cols["gf"], cols["gb"], pad, cols["q"], cols["k"], cols["v"], cols["r"]], axis=1)
    w2 = jnp.concatenate([cols["cb"], cols["cc"], cols["cx"]], axis=1)
    w4 = jnp.concatenate([cols["cq"], cols["ckv"], cols["kr"], _swap_rope_cols(cols["kr"])], axis=1)
    dw = jnp.zeros((128, 2 * GKEY), F32)
    dw = dw.at[:GRANK, :GKEY].set(gla_decay_w[l, 0]).at[GRANK:2 * GRANK, GKEY:].set(gla_decay_w[l, 1])
    db = jnp.concatenate([gla_decay_b[l, 0], gla_decay_b[l, 1]])[None, :]
    wq = mla_w_uq[l].reshape(MQR, MH, MNOPE + MROPE)
    wq = jnp.concatenate([wq, _swap_rope_cols(wq[..., MNOPE:])], axis=-1).reshape(MQR, MH * MHD)
    wkv = mla_w_ukv[l].reshape(MKVR, MH, MNOPE + MV)
    wk = wkv[..., :MNOPE].reshape(MKVR, MH * MNOPE)
    wvt = wkv[..., MNOPE:].reshape(MKVR, MH * MV).T
    bf = lambda a: a.astype(BF16)
    return dict(w1=bf(w1), w2=bf(w2), w3=bf(cols["zg"]), w4=bf(w4), dw=bf(dw), db=db, wq=bf(wq),
                wk=bf(wk), wvt=bf(wvt))


def kernel(x, c, ctx, c_ctx, ada_w, ada_b, ln_w, ln_b, ffn_w13, ffn_w2, w_in, b_gate, gla_decay_w,
           gla_decay_b, gla_norm, gla_proj, conv_w, conv_proj, mla_q_norm, mla_w_uq, mla_kv_norm,
           mla_w_ukv, mla_proj, w_out):
    assert x.shape == (B, L, D) and ctx.shape == (B, LC, D)
    c_all = jnp.concatenate([c, c_ctx[None, :], jnp.zeros((MODROWS - B - 1, D), F32)], axis=0)
    mods = _ada_call(c_all, ada_w, ada_b).reshape(DEPTH, MODROWS, NMOD, 1, D)
    cs = _rope_tables()
    h = (x.reshape(NX, D), ctx.reshape(NC, D))
    vec = lambda a: a.reshape(1, -1)
    for l in range(DEPTH):
        last = l == DEPTH - 1
        rows = NX if last else NT
        m = [mods[l, :, k] for k in range(NMOD)]
        lw = _layer_weights(l, w_in, gla_decay_w, gla_decay_b, mla_w_uq, mla_w_ukv)
        x1, hm = _ffn_call(h, NT, m[0:3], ffn_w13[l, 0].astype(BF16), ffn_w2[l, 0].astype(BF16),
                           vec(ln_w[l, 0]), vec(ln_b[l, 0]), mod_next=(m[3], m[4]))
        qk, v, r, lfb = _p1_call(hm, lw["w1"], lw["dw"], lw["db"])
        zb, u = _p2_call(hm, lw["w2"], rows)
        gates = _p3_call(hm, lw["w3"], vec(b_gate[l]), rows)
        qa, ka, vta = _p4_call(hm, lw["w4"], cs, vec(mla_q_norm[l]), lw["wq"], vec(mla_kv_norm[l]),
                               lw["wk"], lw["wvt"])
        o_f, o_b = _gla_call(qk, v, lfb)
        c_pre = _attn_call(qa, ka, vta, with_ctx=not last)
        x2 = _merge_call(x1, o_f, o_b, r, vec(gla_norm[l]), zb, u, conv_w[l], c_pre, gates,
                         gla_proj[l].astype(BF16), conv_proj[l].astype(BF16), mla_proj[l].astype(BF16),
                         w_out[l].astype(BF16), m[5], vec(ln_w[l, 1]), vec(ln_b[l, 1]), rows)
        h = _ffn_call(x2, rows, m[6:9], ffn_w13[l, 1].astype(BF16), ffn_w2[l, 1].astype(BF16),
                      vec(ln_w[l, 2]), vec(ln_b[l, 2]))
    return h.reshape(B, L, D)
```

```python
import functools

import jax
import jax.numpy as jnp
from jax import lax
from jax.experimental import pallas as pl
from jax.experimental.pallas import tpu as pltpu

F32 = jnp.float32
BF16 = jnp.bfloat16

D = 1024
B = 8
L = 4096
DEPTH = 4
LC = 256
GRID_W = 64
DFF = 2816
GH, GDK, GDV = 4, 128, 256
GKEY, GVAL = GH * GDK, GH * GDV
GRANK = 16
GTEMP = 16.0
GCHUNK = 64
CW = 1024
MH, MQR, MKVR, MNOPE, MROPE, MV = 8, 512, 256, 128, 64, 128
THETA = 10000.0
NMOD = 9
EPS = 1e-6
ALPHA = (2.0 * DEPTH) ** 0.25

NX = B * L
NC = B * LC
NT = NX + NC
MODROWS = 16
MHD = 256
MVA = MV + 16

VMEM_LIMIT = 56 * 1024 * 1024


def _cparams(sem):
    return pltpu.CompilerParams(dimension_semantics=sem, vmem_limit_bytes=VMEM_LIMIT)


def _const(shape):
    n = len(shape)
    return pl.BlockSpec(shape, lambda *_: (0,) * n, pipeline_mode=pl.Buffered(1))


def _mod_spec(tm):
    per = L // tm
    return pl.BlockSpec((None, 1, D), lambda i: (jnp.minimum(i // per, B), 0, 0))


def _silu(x):
    return x * jax.nn.sigmoid(x)


def _layer_norm(z, w, b):
    mu = jnp.mean(z, axis=-1, keepdims=True)
    zc = z - mu
    var = jnp.mean(zc * zc, axis=-1, keepdims=True)
    return zc * lax.rsqrt(var + EPS) * w + b


def _rms(x, w):
    return x * lax.rsqrt(jnp.mean(x * x, axis=-1, keepdims=True) + EPS) * w


def _dot(a, b):
    return jnp.dot(a, b, preferred_element_type=F32)


def _dot_nt(a, b):
    return lax.dot_general(a, b, (((1,), (1,)), ((), ())), preferred_element_type=F32)


def _dot_tn(a, b):
    return lax.dot_general(a, b, (((0,), (0,)), ((), ())), preferred_element_type=F32)


def _ada_kernel(c_ref, w_ref, b_ref, o_ref):
    sc = _silu(c_ref[...]).astype(BF16)
    o_ref[...] = _dot(sc, w_ref[...].astype(BF16)) + b_ref[...]


def _ada_call(c_all, ada_w, ada_b):
    tn = 1152
    return pl.pallas_call(
        _ada_kernel,
        grid=(DEPTH, NMOD * D // tn),
        in_specs=[
            pl.BlockSpec((MODROWS, D), lambda l, j: (0, 0)),
            pl.BlockSpec((None, D, tn), lambda l, j: (l, 0, j)),
            pl.BlockSpec((None, 1, tn), lambda l, j: (l, 0, j)),
        ],
        out_specs=pl.BlockSpec((None, MODROWS, tn), lambda l, j: (l, 0, j)),
        out_shape=jax.ShapeDtypeStruct((DEPTH, MODROWS, NMOD * D), F32),
        compiler_params=_cparams(("arbitrary", "arbitrary")),
        name="ada",
    )(c_all, ada_w, ada_b.reshape(DEPTH, 1, NMOD * D))


SUB_ROWS = 256
PROJ_TM = 1024

def _ffn_kernel(x_ref, *rest, emit_mod, split_tiles):
    if split_tiles:
        xc_ref, *rest = rest
        is_latent = pl.program_id(0) < split_tiles
    sh_ref, sc_ref, g_ref, w13_ref, w2_ref, lw_ref, lb_ref, *rest = rest
    for r0 in range(0, x_ref.shape[0], SUB_ROWS):
        rs = slice(r0, r0 + SUB_ROWS)
        x = x_ref[rs, :]
        if split_tiles:
            x = jnp.where(is_latent, x, xc_ref[rs, :])
        hm = (x * (1.0 + sc_ref[...]) + sh_ref[...]).astype(BF16)
        a = _dot(hm, w13_ref[:, :DFF])
        g = _dot(hm, w13_ref[:, DFF:])
        act = (_silu(a) * g).astype(BF16)
        y = _dot(act, w2_ref[...])
        z = ALPHA * x + (0.5 * g_ref[...]) * y
        out = _layer_norm(z, lw_ref[...], lb_ref[...])
        if emit_mod:
            sh2_ref, sc2_ref, o_ref, om_ref = rest
            o_ref[rs, :] = out
            om_ref[rs, :] = (out * (1.0 + sc2_ref[...]) + sh2_ref[...]).astype(BF16)
        else:
            (o_ref,) = rest
            o_ref[rs, :] = out


def _ffn_call(h, rows, mod3, w13, w2, lw, lb, mod_next=None):
    tm = 1024
    emit = mod_next is not None
    row = pl.BlockSpec((tm, D), lambda i: (i, 0))
    if isinstance(h, tuple):
        nxt = NX // tm
        tokens = list(h)
        token_specs = [pl.BlockSpec((tm, D), lambda i: (jnp.minimum(i, nxt - 1), 0)),
                       pl.BlockSpec((tm, D), lambda i: (jnp.maximum(i - nxt, 0), 0))]
    else:
        nxt = 0
        tokens = [h]
        token_specs = [row]
    in_specs = token_specs + [_mod_spec(tm), _mod_spec(tm), _mod_spec(tm),
                              _const((D, 2 * DFF)), _const((DFF, D)), _const((1, D)), _const((1, D))]
    args = [*tokens, *mod3, w13, w2, lw, lb]
    out_shape = [jax.ShapeDtypeStruct((rows, D), F32)]
    out_specs = [row]
    if emit:
        in_specs += [_mod_spec(tm), _mod_spec(tm)]
        args += list(mod_next)
        out_shape.append(jax.ShapeDtypeStruct((rows, D), BF16))
        out_specs.append(row)
    res = pl.pallas_call(
        functools.partial(_ffn_kernel, emit_mod=emit, split_tiles=nxt),
        grid=(rows // tm,),
        in_specs=in_specs,
        out_specs=out_specs,
        out_shape=out_shape,
        compiler_params=_cparams(("arbitrary",)),
        name="ffn",
    )(*args)
    return res if emit else res[0]


def _subtiles(ref, rows=SUB_ROWS):
    return [slice(r0, r0 + rows) for r0 in range(0, ref.shape[0], rows)]


def _p1_kernel(hm_ref, w_ref, dw_ref, db_ref, qk_ref, v_ref, r_ref, lfb_ref):
    n0 = 128 + 2 * GKEY
    for rs in _subtiles(hm_ref):
        hm = hm_ref[rs, :]
        z0 = _dot(hm, w_ref[:, :n0])
        xg = _dot(z0[:, :128].astype(BF16), dw_ref[...]) + db_ref[...]
        lfb_ref[rs, :] = (jnp.minimum(xg, 0.0) - jnp.log(1.0 + jnp.exp(-jnp.abs(xg)))) * (1.0 / GTEMP)
        qk_ref[rs, :GKEY] = (z0[:, 128:128 + GKEY] * (GDK ** -0.5)).astype(BF16)
        qk_ref[rs, GKEY:] = z0[:, 128 + GKEY:].astype(BF16)
        z1 = _dot(hm, w_ref[:, n0:])
        v_ref[rs, :] = z1[:, :GVAL].astype(BF16)
        r_ref[rs, :] = _silu(z1[:, GVAL:]).astype(BF16)


def _p1_call(hm, w, dw, db):
    tm = PROJ_TM
    return pl.pallas_call(
        _p1_kernel,
        grid=(NT // tm,),
        in_specs=[pl.BlockSpec((tm, D), lambda i: (i, 0)),
                  _const((D, 128 + 2 * GKEY + 2 * GVAL)),
                  _const((128, 2 * GKEY)), _const((1, 2 * GKEY))],
        out_specs=[pl.BlockSpec((tm, 2 * GKEY), lambda i: (i, 0)),
                   pl.BlockSpec((tm, GVAL), lambda i: (i, 0)),
                   pl.BlockSpec((tm, GVAL), lambda i: (i, 0)),
                   pl.BlockSpec((tm, 2 * GKEY), lambda i: (i, 0))],
        out_shape=[jax.ShapeDtypeStruct((NT, 2 * GKEY), BF16),
                   jax.ShapeDtypeStruct((NT, GVAL), BF16),
                   jax.ShapeDtypeStruct((NT, GVAL), BF16),
                   jax.ShapeDtypeStruct((NT, 2 * GKEY), F32)],
        compiler_params=_cparams(("arbitrary",)),
        name="proj_gla",
    )(hm, w, dw, db)


def _p2_kernel(hm_ref, w_ref, zb_ref, u_ref):
    for rs in _subtiles(hm_ref):
        z = _dot(hm_ref[rs, :], w_ref[...])
        zb_ref[rs, :] = z[:, :CW].astype(BF16)
        u_ref[rs, :] = (z[:, CW:2 * CW] * z[:, 2 * CW:]).astype(BF16)


def _p2_call(hm, w, rows):
    tm = PROJ_TM
    return pl.pallas_call(
        _p2_kernel,
        grid=(rows // tm,),
        in_specs=[pl.BlockSpec((tm, D), lambda i: (i, 0)), _const((D, 3 * CW))],
        out_specs=[pl.BlockSpec((tm, CW), lambda i: (i, 0)),
                   pl.BlockSpec((tm, CW), lambda i: (i, 0))],
        out_shape=[jax.ShapeDtypeStruct((rows, CW), BF16),
                   jax.ShapeDtypeStruct((rows, CW), BF16)],
        compiler_params=_cparams(("arbitrary",)),
        name="proj_conv",
    )(hm, w)


def _p3_kernel(hm_ref, w_ref, b_ref, g_ref):
    for rs in _subtiles(hm_ref):
        z = _dot(hm_ref[rs, :], w_ref[...]) + b_ref[...]
        g_ref[rs, :] = jax.nn.sigmoid(z).astype(BF16)


def _p3_call(hm, w, b_gate, rows):
    tm = PROJ_TM
    return pl.pallas_call(
        _p3_kernel,
        grid=(rows // tm,),
        in_specs=[pl.BlockSpec((tm, D), lambda i: (i, 0)), _const((D, 3 * D)), _const((1, 3 * D))],
        out_specs=pl.BlockSpec((tm, 3 * D), lambda i: (i, 0)),
        out_shape=jax.ShapeDtypeStruct((rows, 3 * D), BF16),
        compiler_params=_cparams(("arbitrary",)),
        name="proj_gate",
    )(hm, w, b_gate)


def _rope_fold(t, cs):
    rr = t * cs
    return rr + pltpu.roll(rr, MROPE, 1)


Q_SCALE = (MNOPE + MROPE) ** -0.5 * 1.4426950408889634
MLA_SUB = 512


def _p4_kernel(hm_ref, w_ref, cs_ref, qn_ref, wuq_ref, kvn_ref, wuk_ref, wuvt_ref, q_ref, k_ref, vt_ref):
    for rs in _subtiles(hm_ref, MLA_SUB):
        z = _dot(hm_ref[rs, :], w_ref[...])
        cs = cs_ref[rs, :]
        cqn = _rms(z[:, :MQR], qn_ref[...]).astype(BF16)
        kvn = _rms(z[:, MQR:MQR + MKVR], kvn_ref[...]).astype(BF16)
        qz = _dot(cqn, wuq_ref[...]) * Q_SCALE
        kz = _dot(kvn, wuk_ref[...])
        vt = _dot_nt(wuvt_ref[...], kvn)
        for h in range(MH):
            base = h * MHD
            rot = _rope_fold(qz[:, base + MNOPE:base + MHD], cs)
            q_ref[h, rs, :] = jnp.concatenate([qz[:, base:base + MNOPE], rot], axis=1).astype(BF16)
        krot = _rope_fold(z[:, MQR + MKVR:], cs)
        lane = lax.broadcasted_iota(jnp.int32, krot.shape, 1)
        krot = jnp.where(lane < MROPE, krot, 0.0)
        for h in range(MH):
            k_ref[h, rs, :] = jnp.concatenate([kz[:, h * MNOPE:(h + 1) * MNOPE], krot], axis=1).astype(BF16)
            vt_ref[h, :MV, rs] = vt[h * MV:(h + 1) * MV, :].astype(BF16)
            vt_ref[h, MV:, rs] = jnp.ones((MVA - MV, MLA_SUB), BF16)


def _p4_call(hm, w, cs, qn, wuq, kvn, wuk, wuvt):
    tm = PROJ_TM
    return pl.pallas_call(
        _p4_kernel,
        grid=(NT // tm,),
        in_specs=[pl.BlockSpec((tm, D), lambda i: (i, 0)), _const((D, MQR + MKVR + 2 * MROPE)),
                  pl.BlockSpec((tm, 2 * MROPE), lambda i: (i, 0)),
                  _const((1, MQR)), _const((MQR, MH * MHD)),
                  _const((1, MKVR)), _const((MKVR, MH * MNOPE)), _const((MH * MV, MKVR))],
        out_specs=[pl.BlockSpec((MH, tm, MHD), lambda i: (0, i, 0)),
                   pl.BlockSpec((MH, tm, MHD), lambda i: (0, i, 0)),
                   pl.BlockSpec((MH, MVA, tm), lambda i: (0, 0, i))],
        out_shape=[jax.ShapeDtypeStruct((MH, NT, MHD), BF16),
                   jax.ShapeDtypeStruct((MH, NT, MHD), BF16),
                   jax.ShapeDtypeStruct((MH, MVA, NT), BF16)],
        compiler_params=_cparams(("arbitrary",)),
        name="proj_mla",
    )(hm, w, cs, qn, wuq, kvn, wuk, wuvt)


GLA_TILE = 256
GLA_NCH = GLA_TILE // GCHUNK
GLA_NB = 2
assert LC == GLA_TILE


def _gla_local(qk_ref, v_ref, g_ref, backward):
    g = g_ref[...]
    ri = lax.broadcasted_iota(jnp.int32, (GLA_TILE, GLA_TILE), 0)
    ci = lax.broadcasted_iota(jnp.int32, (GLA_TILE, GLA_TILE), 1)
    same = (ri // GCHUNK) == (ci // GCHUNK)
    tri = jnp.where(same & ((ci >= ri) if backward else (ci <= ri)), 1.0, 0.0).astype(BF16)
    g_hi = g.astype(BF16)
    g_lo = (g - g_hi.astype(F32)).astype(BF16)
    bc = _dot(tri, g_hi) + _dot(tri, g_lo)
    r64 = lax.broadcasted_iota(jnp.int32, (GCHUNK, GCHUNK), 0)
    c64 = lax.broadcasted_iota(jnp.int32, (GCHUNK, GCHUNK), 1)
    mask = (c64 >= r64) if backward else (c64 <= r64)
    order = range(GLA_NCH - 1, -1, -1) if backward else range(GLA_NCH)
    items = [dict(r0=c * GCHUNK, h=h) for c in order for h in range(GH)]

    def decay_and_intra(it):
        r0, h = it["r0"], it["h"]
        b = bc[r0:r0 + GCHUNK, h * GDK:(h + 1) * GDK]
        bl = b[0:1] if backward else b[GCHUNK - 1:GCHUNK]
        q = qk_ref[r0:r0 + GCHUNK, h * GDK:(h + 1) * GDK].astype(F32)
        k = qk_ref[r0:r0 + GCHUNK, GKEY + h * GDK:GKEY + (h + 1) * GDK].astype(F32)
        qe = (q * jnp.exp(b)).astype(BF16)
        ke = (k * jnp.exp(-b)).astype(BF16)
        it.update(qe=qe, kd=(k * jnp.exp(bl - b)).astype(BF16), dec=jnp.exp(bl),
                  a=jnp.where(mask, _dot_nt(qe, ke), 0.0).astype(BF16))

    def value_products(it):
        v = v_ref[it["r0"]:it["r0"] + GCHUNK, it["h"] * GDV:(it["h"] + 1) * GDV]
        it["oi"] = _dot(it.pop("a"), v)
        it["u"] = _dot_tn(v, it.pop("kd"))

    steps = [functools.partial(decay_and_intra, it) for it in items]
    steps += [functools.partial(value_products, it) for it in items]
    return items, steps


def _gla_advance(scans, state):
    def sequential(i, items, store):
        def step(it):
            h = it["h"]
            s = state[i][h]
            store(it["r0"], h, it["oi"] + _dot_nt(it["qe"], s.astype(BF16)))
            state[i][h] = s * it["dec"] + it["u"]
        return [functools.partial(step, it) for it in items]

    pending = []
    for i, (qk_ref, v_ref, g_ref, backward, store) in enumerate(scans):
        items, local = _gla_local(qk_ref, v_ref, g_ref, backward)
        per = -(-len(local) // max(len(pending), 1))
        while pending or local:
            if pending:
                pending.pop(0)()
            for _ in range(min(per, len(local))):
                local.pop(0)()
        pending = sequential(i, items, store)
    for step in pending:
        step()
    return state


def _gla_store(o_ref, lead=()):
    def store(r0, h, val):
        o_ref[(*lead, slice(r0, r0 + GCHUNK), slice(h * GDV, (h + 1) * GDV))] = val
    return store


def _gla_ctx_kernel(qk_ref, v_ref, gf_ref, gb_ref, of_ref, ob_ref, s_ref):
    scans = [(qk_ref, v_ref, gf_ref, False, _gla_store(of_ref)),
             (qk_ref, v_ref, gb_ref, True, _gla_store(ob_ref))]
    zero = jnp.zeros((GDV, GDK), F32)
    state = _gla_advance(scans, [[zero] * GH for _ in scans])
    for d in range(2):
        for h in range(GH):
            s_ref[d, h] = state[d][h]


def _gla_x_kernel(*refs):
    ins, (s0_ref, of_ref, ob_ref, s_ref) = refs[:6 * GLA_NB], refs[6 * GLA_NB:]

    @pl.when(pl.program_id(1) == 0)
    def _():
        s_ref[...] = s0_ref[...]

    scans, slots = [], []
    for nb in range(GLA_NB):
        qkf, vf, gf, qkb, vb, gb = ins[6 * nb:6 * nb + 6]
        scans.append((qkf, vf, gf, False, _gla_store(of_ref, (nb,))))
        scans.append((qkb, vb, gb, True, _gla_store(ob_ref, (nb,))))
        slots += [(nb, 0), (nb, 1)]
    state = _gla_advance(scans, [[s_ref[nb, d, h] for h in range(GH)] for nb, d in slots])
    for (nb, d), st in zip(slots, state):
        for h in range(GH):
            s_ref[nb, d, h] = st[h]


def _gla_call(qk, v, lfb):
    ctx_blk = NX // LC
    of_c, ob_c, s_ctx = pl.pallas_call(
        _gla_ctx_kernel,
        grid=(B,),
        in_specs=[pl.BlockSpec((LC, 2 * GKEY), lambda b: (ctx_blk + b, 0)),
                  pl.BlockSpec((LC, GVAL), lambda b: (ctx_blk + b, 0)),
                  pl.BlockSpec((LC, GKEY), lambda b: (ctx_blk + b, 0)),
                  pl.BlockSpec((LC, GKEY), lambda b: (ctx_blk + b, 1))],
        out_specs=[pl.BlockSpec((LC, GVAL), lambda b: (b, 0)),
                   pl.BlockSpec((LC, GVAL), lambda b: (b, 0)),
                   pl.BlockSpec((None, 2, GH, GDV, GDK), lambda b: (b, 0, 0, 0, 0))],
        out_shape=[jax.ShapeDtypeStruct((NC, GVAL), F32), jax.ShapeDtypeStruct((NC, GVAL), F32),
                   jax.ShapeDtypeStruct((B, 2, GH, GDV, GDK), F32)],
        compiler_params=_cparams(("arbitrary",)),
        name="gla_ctx",
    )(qk, v, lfb, lfb)

    nxt = L // GLA_TILE
    in_specs, args = [], []
    for nb in range(GLA_NB):
        for d, tile in ((0, lambda t: t), (1, lambda t: nxt - 1 - t)):
            blk = lambda k, t, nb=nb, tile=tile: (k * GLA_NB + nb) * nxt + tile(t)
            in_specs += [pl.BlockSpec((GLA_TILE, 2 * GKEY), lambda k, t, blk=blk: (blk(k, t), 0)),
                         pl.BlockSpec((GLA_TILE, GVAL), lambda k, t, blk=blk: (blk(k, t), 0)),
                         pl.BlockSpec((GLA_TILE, GKEY), lambda k, t, blk=blk, d=d: (blk(k, t), d))]
            args += [qk, v, lfb]
    in_specs.append(pl.BlockSpec((GLA_NB, 2, GH, GDV, GDK), lambda k, t: (k, 0, 0, 0, 0)))
    of_x, ob_x = pl.pallas_call(
        _gla_x_kernel,
        grid=(B // GLA_NB, nxt),
        in_specs=in_specs,
        out_specs=[pl.BlockSpec((GLA_NB, GLA_TILE, GVAL), lambda k, t: (k, t, 0)),
                   pl.BlockSpec((GLA_NB, GLA_TILE, GVAL), lambda k, t: (k, nxt - 1 - t, 0))],
        out_shape=[jax.ShapeDtypeStruct((B, L, GVAL), F32), jax.ShapeDtypeStruct((B, L, GVAL), F32)],
        scratch_shapes=[pltpu.VMEM((GLA_NB, 2, GH, GDV, GDK), F32)],
        compiler_params=_cparams(("arbitrary", "arbitrary")),
        name="gla",
    )(*args, s_ctx)
    return of_x.reshape(NX, GVAL), ob_x.reshape(NX, GVAL), of_c, ob_c


ATT_TQ = 1024
ATT_KC = 1024
ATT_AHEAD = 2
ATT_SLACK = 64.0
ATT_HP = 2


def _flash_t(streams, lazy_max, ahead_n):
    n = len(streams[0][1])
    m = [None] * len(streams)
    acc = [None] * len(streams)
    excess = [None] * len(streams)
    ahead = [[_dot_nt(ch[j][0], q) for j in range(min(ahead_n, n))] for q, ch in streams]
    for j in range(n):
        for i, (q, ch) in enumerate(streams):
            s = ahead[i].pop(0)
            if j + ahead_n < n:
                ahead[i].append(_dot_nt(ch[j + ahead_n][0], q))
            vt = ch[j][1]
            cm = jnp.max(s, axis=0, keepdims=True)
            if j == 0:
                m[i] = cm
                acc[i] = _dot(vt, jnp.exp2(s - cm).astype(BF16))
                continue
            m_new = jnp.maximum(m[i], cm)
            if lazy_max:
                over = cm - m[i]
                excess[i] = over if excess[i] is None else jnp.maximum(excess[i], over)
                acc[i] = (acc[i] + _dot(vt, jnp.exp2(s - m[i]).astype(BF16))) * jnp.exp2(m[i] - m_new)
            else:
                acc[i] = jnp.exp2(m[i] - m_new) * acc[i] + _dot(vt, jnp.exp2(s - m_new).astype(BF16))
            m[i] = m_new
    return [a[:MV, :] / a[MV:MV + 1, :] for a in acc], excess


def _attn_kernel(q_ref, kx_ref, kc_ref, vtx_ref, vtc_ref, o_ref):
    def run(lazy_max, ahead_n):
        streams = []
        for h in range(ATT_HP):
            chunks = [(kc_ref[h], vtc_ref[h])]
            for j in range(L // ATT_KC):
                ks = slice(j * ATT_KC, (j + 1) * ATT_KC)
                chunks.append((kx_ref[h, ks, :], vtx_ref[h, :, ks]))
            streams.append((q_ref[h], chunks))
        outs, excess = _flash_t(streams, lazy_max, ahead_n)
        for h, o in enumerate(outs):
            o_ref[:, h * MV:(h + 1) * MV] = o.T.astype(BF16)
        return excess

    excess = run(True, 1)
    worst = functools.reduce(jnp.maximum, [jnp.max(e) for e in excess])

    @pl.when(worst > ATT_SLACK)
    def _():
        run(False, ATT_AHEAD)


def _attn_ctx_kernel(q_ref, kc_ref, vtc_ref, prev_ref, o_ref):
    del prev_ref
    outs, _ = _flash_t([(q_ref[h], [(kc_ref[h], vtc_ref[h])]) for h in range(MH)], False, 1)
    for h, o in enumerate(outs):
        o_ref[:, h * MV:(h + 1) * MV] = o.T.astype(BF16)


def _attn_call(qa, ka, vta, with_ctx):
    nq = L // ATT_TQ
    hp = ATT_HP
    out = pl.pallas_call(
        _attn_kernel,
        grid=(B, MH // hp, nq),
        in_specs=[pl.BlockSpec((hp, ATT_TQ, MHD), lambda b, h, i: (h, b * nq + i, 0)),
                  pl.BlockSpec((hp, L, MHD), lambda b, h, i: (h, b, 0)),
                  pl.BlockSpec((hp, LC, MHD), lambda b, h, i: (h, NX // LC + b, 0)),
                  pl.BlockSpec((hp, MVA, L), lambda b, h, i: (h, 0, b)),
                  pl.BlockSpec((hp, MVA, LC), lambda b, h, i: (h, 0, NX // LC + b))],
        out_specs=pl.BlockSpec((ATT_TQ, hp * MV), lambda b, h, i: (b * nq + i, h)),
        out_shape=jax.ShapeDtypeStruct((NT if with_ctx else NX, MH * MV), BF16),
        compiler_params=_cparams(("arbitrary", "arbitrary", "arbitrary")),
        name="attn",
    )(qa, ka, ka, vta, vta)
    if not with_ctx:
        return out
    return pl.pallas_call(
        _attn_ctx_kernel,
        grid=(B,),
        in_specs=[pl.BlockSpec((MH, LC, MHD), lambda b: (0, NX // LC + b, 0)),
                  pl.BlockSpec((MH, LC, MHD), lambda b: (0, NX // LC + b, 0)),
                  pl.BlockSpec((MH, MVA, LC), lambda b: (0, 0, NX // LC + b)),
                  pl.BlockSpec(memory_space=pl.ANY)],
        out_specs=pl.BlockSpec((LC, MH * MV), lambda b: (NX // LC + b, 0)),
        out_shape=jax.ShapeDtypeStruct((NT, MH * MV), BF16),
        input_output_aliases={3: 0},
        compiler_params=_cparams(("arbitrary",)),
        name="attn_ctx",
    )(qa, ka, vta, out)


MERGE_TM = 512


def _merge_kernel(x_ref, ofx_ref, obx_ref, ofc_ref, obc_ref, r_ref, gn_ref, zb_ref, u_ref, up_ref, un_ref,
                  cw_ref, c_ref, gt_ref, wa_ref, wb_ref, wc_ref, wo_ref, g5_ref, lw_ref, lb_ref, o_ref):
    tm = MERGE_TM
    c = _dot(c_ref[...], wc_ref[...])
    o = jnp.where(pl.program_id(0) < NX // tm, ofx_ref[...] + obx_ref[...], ofc_ref[...] + obc_ref[...])
    parts = []
    for h in range(GH):
        oh = o[:, h * GDV:(h + 1) * GDV]
        parts.append(oh * lax.rsqrt(jnp.mean(oh * oh, axis=-1, keepdims=True) + EPS))
    a_in = (jnp.concatenate(parts, axis=1) * gn_ref[...] * r_ref[...].astype(F32)).astype(BF16)
    u = u_ref[...].astype(F32)
    prev = up_ref[7:8, :].astype(F32)
    nxt = un_ref[0:1, :].astype(F32)
    loc = lax.broadcasted_iota(jnp.int32, (tm, 1), 0)
    row = loc + pl.program_id(0) * tm
    seq = jnp.where(row >= NX, LC, L)
    pos = row & (seq - 1)
    ul = jnp.where(loc == 0, prev, pltpu.roll(u, 1, 0))
    ul = jnp.where(pos == 0, 0.0, ul)
    ur = jnp.where(loc == tm - 1, nxt, pltpu.roll(u, tm - 1, 0))
    ur = jnp.where(pos == seq - 1, 0.0, ur)
    cv = ul * cw_ref[0:1, :] + u * cw_ref[1:2, :] + ur * cw_ref[2:3, :]
    b_in = (zb_ref[...].astype(F32) * cv).astype(BF16)
    subs = _subtiles(x_ref)
    ab = [(_dot(a_in[rs, :], wa_ref[...]), _dot(b_in[rs, :], wb_ref[...])) for rs in subs]
    for rs, (a, bb) in zip(subs, ab):
        mix = (gt_ref[rs, :D].astype(F32) * a + gt_ref[rs, D:2 * D].astype(F32) * bb
               + gt_ref[rs, 2 * D:].astype(F32) * c[rs, :])
        mx = _dot(mix.astype(BF16), wo_ref[...])
        z = ALPHA * x_ref[rs, :] + g5_ref[...] * mx
        o_ref[rs, :] = _layer_norm(z, lw_ref[...], lb_ref[...])


def _merge_call(x1, o_gla, r, gnorm, zb, u, conv_w, c_pre, gates, wa, wb, wc, wo, g5, lw, lb, rows):
    tm = MERGE_TM
    row = lambda w: pl.BlockSpec((tm, w), lambda i: (i, 0))
    nxt = NX // tm
    lat = pl.BlockSpec((tm, GVAL), lambda i: (jnp.minimum(i, nxt - 1), 0))
    ctx = pl.BlockSpec((tm, GVAL), lambda i: (jnp.maximum(i - nxt, 0), 0))
    last8 = rows // 8 - 1
    return pl.pallas_call(
        _merge_kernel,
        grid=(rows // tm,),
        in_specs=[row(D), lat, lat, ctx, ctx, row(GVAL), _const((1, GVAL)),
                  row(CW), row(CW),
                  pl.BlockSpec((8, CW), lambda i: (jnp.maximum(i * (tm // 8) - 1, 0), 0)),
                  pl.BlockSpec((8, CW), lambda i: (jnp.minimum((i + 1) * (tm // 8), last8), 0)),
                  _const((3, CW)),
                  row(MH * MV), row(3 * D),
                  _const((GVAL, D)), _const((CW, D)), _const((MH * MV, D)), _const((D, D)),
                  _mod_spec(tm), _const((1, D)), _const((1, D))],
        out_specs=row(D),
        out_shape=jax.ShapeDtypeStruct((rows, D), F32),
        compiler_params=_cparams(("arbitrary",)),
        name="merge",
    )(x1, *o_gla, r, gnorm, zb, u, u, u, conv_w, c_pre, gates, wa, wb, wc, wo, g5, lw, lb)


def _rope_tables():
    t = jnp.arange(L, dtype=jnp.int32)
    quarter = MROPE // 4
    inv_freq = THETA ** (-jnp.arange(quarter, dtype=F32) / quarter)

    def cs(pos):
        ang = pos.astype(F32)[:, None] * inv_freq
        return jnp.cos(ang), jnp.sin(ang)

    cr, sr = cs(t // GRID_W)
    cc, sn = cs(t % GRID_W)
    tab = jnp.concatenate([cr, cr, cc, cc, -sr, sr, -sn, sn], axis=1)
    ident = jnp.concatenate([jnp.ones((NC, MROPE), F32), jnp.zeros((NC, MROPE), F32)], axis=1)
    return jnp.concatenate([jnp.tile(tab, (B, 1)), ident], axis=0)


def _swap_rope_cols(w):
    q = MROPE // 4
    return jnp.concatenate([w[..., q:2 * q], w[..., :q], w[..., 3 * q:], w[..., 2 * q:3 * q]], axis=-1)


def _layer_weights(l, w_in, gla_decay_w, gla_decay_b, mla_w_uq, mla_w_ukv):
    wi = w_in[l]
    o = 0
    cols = {}
    for name, n in (("q", GKEY), ("k", GKEY), ("v", GVAL), ("r", GVAL), ("gf", GRANK), ("gb", GRANK),
                    ("cb", CW), ("cc", CW), ("cx", CW), ("cq", MQR), ("ckv", MKVR), ("kr", MROPE),
                    ("zg", 3 * D)):
        cols[name] = wi[:, o:o + n]
        o += n
    pad = jnp.zeros((D, 128 - 2 * GRANK), F32)
    w1 = jnp.concatenate([cols["gf"], cols["gb"], pad, cols["q"], cols["k"], cols["v"], cols["r"]], axis=1)
    w2 = jnp.concatenate([cols["cb"], cols["cc"], cols["cx"]], axis=1)
    w4 = jnp.concatenate([cols["cq"], cols["ckv"], cols["kr"], _swap_rope_cols(cols["kr"])], axis=1)
    dw = jnp.zeros((128, 2 * GKEY), F32)
    dw = dw.at[:GRANK, :GKEY].set(gla_decay_w[l, 0]).at[GRANK:2 * GRANK, GKEY:].set(gla_decay_w[l, 1])
    db = jnp.concatenate([gla_decay_b[l, 0], gla_decay_b[l, 1]])[None, :]
    wq = mla_w_uq[l].reshape(MQR, MH, MNOPE + MROPE)
    wq = jnp.concatenate([wq, _swap_rope_cols(wq[..., MNOPE:])], axis=-1).reshape(MQR, MH * MHD)
    wkv = mla_w_ukv[l].reshape(MKVR, MH, MNOPE + MV)
    wk = wkv[..., :MNOPE].reshape(MKVR, MH * MNOPE)
    wvt = wkv[..., MNOPE:].reshape(MKVR, MH * MV).T
    bf = lambda a: a.astype(BF16)
    return dict(w1=bf(w1), w2=bf(w2), w3=bf(cols["zg"]), w4=bf(w4), dw=bf(dw), db=db, wq=bf(wq),
                wk=bf(wk), wvt=bf(wvt))


def kernel(x, c, ctx, c_ctx, ada_w, ada_b, ln_w, ln_b, ffn_w13, ffn_w2, w_in, b_gate, gla_decay_w,
           gla_decay_b, gla_norm, gla_proj, conv_w, conv_proj, mla_q_norm, mla_w_uq, mla_kv_norm,
           mla_w_ukv, mla_proj, w_out):
    assert x.shape == (B, L, D) and ctx.shape == (B, LC, D)
    c_all = jnp.concatenate([c, c_ctx[None, :], jnp.zeros((MODROWS - B - 1, D), F32)], axis=0)
    mods = _ada_call(c_all, ada_w, ada_b).reshape(DEPTH, MODROWS, NMOD, 1, D)
    cs = _rope_tables()
    h = (x.reshape(NX, D), ctx.reshape(NC, D))
    vec = lambda a: a.reshape(1, -1)
    for l in range(DEPTH):
        last = l == DEPTH - 1
        rows = NX if last else NT
        m = [mods[l, :, k] for k in range(NMOD)]
        lw = _layer_weights(l, w_in, gla_decay_w, gla_decay_b, mla_w_uq, mla_w_ukv)
        x1, hm = _ffn_call(h, NT, m[0:3], ffn_w13[l, 0].astype(BF16), ffn_w2[l, 0].astype(BF16),
                           vec(ln_w[l, 0]), vec(ln_b[l, 0]), mod_next=(m[3], m[4]))
        qk, v, r, lfb = _p1_call(hm, lw["w1"], lw["dw"], lw["db"])
        zb, u = _p2_call(hm, lw["w2"], rows)
        gates = _p3_call(hm, lw["w3"], vec(b_gate[l]), rows)
        qa, ka, vta = _p4_call(hm, lw["w4"], cs, vec(mla_q_norm[l]), lw["wq"], vec(mla_kv_norm[l]),
                               lw["wk"], lw["wvt"])
        o_gla = _gla_call(qk, v, lfb)
        c_pre = _attn_call(qa, ka, vta, with_ctx=not last)
        x2 = _merge_call(x1, o_gla, r, vec(gla_norm[l]), zb, u, conv_w[l], c_pre, gates,
                         gla_proj[l].astype(BF16), conv_proj[l].astype(BF16), mla_proj[l].astype(BF16),
                         w_out[l].astype(BF16), m[5], vec(ln_w[l, 1]), vec(ln_b[l, 1]), rows)
        h = _ffn_call(x2, rows, m[6:9], ffn_w13[l, 1].astype(BF16), ffn_w2[l, 1].astype(BF16),
                      vec(ln_w[l, 2]), vec(ln_b[l, 2]))
    return h.reshape(B, L, D)
```

```python
import functools

import jax
import jax.numpy as jnp
from jax import lax
from jax.experimental import pallas as pl
from jax.experimental.pallas import tpu as pltpu

F32 = jnp.float32
BF16 = jnp.bfloat16

D = 1024
B = 8
L = 4096
DEPTH = 4
LC = 256
GRID_W = 64
DFF = 2816
GH, GDK, GDV = 4, 128, 256
GKEY, GVAL = GH * GDK, GH * GDV
GRANK = 16
GTEMP = 16.0
GCHUNK = 64
CW = 1024
MH, MQR, MKVR, MNOPE, MROPE, MV = 8, 512, 256, 128, 64, 128
THETA = 10000.0
NMOD = 9
EPS = 1e-6
ALPHA = (2.0 * DEPTH) ** 0.25

LANES = 128
SUBLANES_F32 = 8
SUBLANES_BF16 = 16
VMEM_LIMIT = 56 * 1024 * 1024

NX = B * L
NC = B * LC
NT = NX + NC
MODROWS = 2 * SUBLANES_F32
MHD = 2 * LANES
MVA = MV + SUBLANES_BF16
GRANK_PAD = LANES
HALO_ROWS = SUBLANES_F32


def _cparams(sem):
    return pltpu.CompilerParams(dimension_semantics=sem, vmem_limit_bytes=VMEM_LIMIT)


def _const(shape):
    n = len(shape)
    return pl.BlockSpec(shape, lambda *_: (0,) * n, pipeline_mode=pl.Buffered(1))


def _mod_spec(tm):
    per = L // tm
    return pl.BlockSpec((None, 1, D), lambda i: (jnp.minimum(i // per, B), 0, 0))


def _silu(x):
    return x * jax.nn.sigmoid(x)


def _layer_norm(z, w, b):
    mu = jnp.mean(z, axis=-1, keepdims=True)
    zc = z - mu
    var = jnp.mean(zc * zc, axis=-1, keepdims=True)
    return zc * lax.rsqrt(var + EPS) * w + b


def _rms(x, w):
    return x * lax.rsqrt(jnp.mean(x * x, axis=-1, keepdims=True) + EPS) * w


def _dot(a, b):
    return jnp.dot(a, b, preferred_element_type=F32)


def _dot_nt(a, b):
    return lax.dot_general(a, b, (((1,), (1,)), ((), ())), preferred_element_type=F32)


def _dot_tn(a, b):
    return lax.dot_general(a, b, (((0,), (0,)), ((), ())), preferred_element_type=F32)


def _ada_kernel(c_ref, w_ref, b_ref, o_ref):
    sc = _silu(c_ref[...]).astype(BF16)
    o_ref[...] = _dot(sc, w_ref[...].astype(BF16)) + b_ref[...]


def _ada_call(c_all, ada_w, ada_b):
    tn = NMOD * LANES
    return pl.pallas_call(
        _ada_kernel,
        grid=(DEPTH, NMOD * D // tn),
        in_specs=[
            pl.BlockSpec((MODROWS, D), lambda l, j: (0, 0)),
            pl.BlockSpec((None, D, tn), lambda l, j: (l, 0, j)),
            pl.BlockSpec((None, 1, tn), lambda l, j: (l, 0, j)),
        ],
        out_specs=pl.BlockSpec((None, MODROWS, tn), lambda l, j: (l, 0, j)),
        out_shape=jax.ShapeDtypeStruct((DEPTH, MODROWS, NMOD * D), F32),
        compiler_params=_cparams(("arbitrary", "arbitrary")),
        name="ada",
    )(c_all, ada_w, ada_b.reshape(DEPTH, 1, NMOD * D))


SUB_ROWS = 256
PROJ_TM = 1024


def _ffn_kernel(x_ref, *rest, emit_mod, split_tiles):
    if split_tiles:
        xc_ref, *rest = rest
        is_latent = pl.program_id(0) < split_tiles
    sh_ref, sc_ref, g_ref, w13_ref, w2_ref, lw_ref, lb_ref, *rest = rest
    for r0 in range(0, x_ref.shape[0], SUB_ROWS):
        rs = slice(r0, r0 + SUB_ROWS)
        x = x_ref[rs, :]
        if split_tiles:
            x = jnp.where(is_latent, x, xc_ref[rs, :])
        hm = (x * (1.0 + sc_ref[...]) + sh_ref[...]).astype(BF16)
        a = _dot(hm, w13_ref[:, :DFF])
        g = _dot(hm, w13_ref[:, DFF:])
        act = (_silu(a) * g).astype(BF16)
        y = _dot(act, w2_ref[...])
        z = ALPHA * x + (0.5 * g_ref[...]) * y
        out = _layer_norm(z, lw_ref[...], lb_ref[...])
        if emit_mod:
            sh2_ref, sc2_ref, o_ref, om_ref = rest
            o_ref[rs, :] = out
            om_ref[rs, :] = (out * (1.0 + sc2_ref[...]) + sh2_ref[...]).astype(BF16)
        else:
            (o_ref,) = rest
            o_ref[rs, :] = out


def _ffn_call(h, rows, mod3, w13, w2, lw, lb, mod_next=None):
    tm = 1024
    emit = mod_next is not None
    row = pl.BlockSpec((tm, D), lambda i: (i, 0))
    if isinstance(h, tuple):
        nxt = NX // tm
        tokens = list(h)
        token_specs = [pl.BlockSpec((tm, D), lambda i: (jnp.minimum(i, nxt - 1), 0)),
                       pl.BlockSpec((tm, D), lambda i: (jnp.maximum(i - nxt, 0), 0))]
    else:
        nxt = 0
        tokens = [h]
        token_specs = [row]
    in_specs = token_specs + [_mod_spec(tm), _mod_spec(tm), _mod_spec(tm),
                              _const((D, 2 * DFF)), _const((DFF, D)), _const((1, D)), _const((1, D))]
    args = [*tokens, *mod3, w13, w2, lw, lb]
    out_shape = [jax.ShapeDtypeStruct((rows, D), F32)]
    out_specs = [row]
    if emit:
        in_specs += [_mod_spec(tm), _mod_spec(tm)]
        args += list(mod_next)
        out_shape.append(jax.ShapeDtypeStruct((rows, D), BF16))
        out_specs.append(row)
    res = pl.pallas_call(
        functools.partial(_ffn_kernel, emit_mod=emit, split_tiles=nxt),
        grid=(rows // tm,),
        in_specs=in_specs,
        out_specs=out_specs,
        out_shape=out_shape,
        compiler_params=_cparams(("arbitrary",)),
        name="ffn",
    )(*args)
    return res if emit else res[0]


def _subtiles(ref, rows=SUB_ROWS):
    return [slice(r0, r0 + rows) for r0 in range(0, ref.shape[0], rows)]


def _p1_kernel(hm_ref, w_ref, dw_ref, db_ref, qk_ref, v_ref, r_ref, lfb_ref):
    n0 = GRANK_PAD + 2 * GKEY
    for rs in _subtiles(hm_ref):
        hm = hm_ref[rs, :]
        z0 = _dot(hm, w_ref[:, :n0])
        xg = _dot(z0[:, :GRANK_PAD].astype(BF16), dw_ref[...]) + db_ref[...]
        lfb_ref[rs, :] = (jnp.minimum(xg, 0.0) - jnp.log(1.0 + jnp.exp(-jnp.abs(xg)))) * (1.0 / GTEMP)
        qk_ref[rs, :GKEY] = (z0[:, GRANK_PAD:GRANK_PAD + GKEY] * (GDK ** -0.5)).astype(BF16)
        qk_ref[rs, GKEY:] = z0[:, GRANK_PAD + GKEY:].astype(BF16)
        z1 = _dot(hm, w_ref[:, n0:])
        v_ref[rs, :] = z1[:, :GVAL].astype(BF16)
        r_ref[rs, :] = _silu(z1[:, GVAL:]).astype(BF16)


def _p1_call(hm, w, dw, db):
    tm = PROJ_TM
    return pl.pallas_call(
        _p1_kernel,
        grid=(NT // tm,),
        in_specs=[pl.BlockSpec((tm, D), lambda i: (i, 0)),
                  _const((D, GRANK_PAD + 2 * GKEY + 2 * GVAL)),
                  _const((GRANK_PAD, 2 * GKEY)), _const((1, 2 * GKEY))],
        out_specs=[pl.BlockSpec((tm, 2 * GKEY), lambda i: (i, 0)),
                   pl.BlockSpec((tm, GVAL), lambda i: (i, 0)),
                   pl.BlockSpec((tm, GVAL), lambda i: (i, 0)),
                   pl.BlockSpec((tm, 2 * GKEY), lambda i: (i, 0))],
        out_shape=[jax.ShapeDtypeStruct((NT, 2 * GKEY), BF16),
                   jax.ShapeDtypeStruct((NT, GVAL), BF16),
                   jax.ShapeDtypeStruct((NT, GVAL), BF16),
                   jax.ShapeDtypeStruct((NT, 2 * GKEY), F32)],
        compiler_params=_cparams(("arbitrary",)),
        name="proj_gla",
    )(hm, w, dw, db)


def _p2_kernel(hm_ref, w_ref, zb_ref, u_ref):
    for rs in _subtiles(hm_ref):
        z = _dot(hm_ref[rs, :], w_ref[...])
        zb_ref[rs, :] = z[:, :CW].astype(BF16)
        u_ref[rs, :] = (z[:, CW:2 * CW] * z[:, 2 * CW:]).astype(BF16)


def _p2_call(hm, w, rows):
    tm = 2 * PROJ_TM
    return pl.pallas_call(
        _p2_kernel,
        grid=(rows // tm,),
        in_specs=[pl.BlockSpec((tm, D), lambda i: (i, 0)), _const((D, 3 * CW))],
        out_specs=[pl.BlockSpec((tm, CW), lambda i: (i, 0)),
                   pl.BlockSpec((tm, CW), lambda i: (i, 0))],
        out_shape=[jax.ShapeDtypeStruct((rows, CW), BF16),
                   jax.ShapeDtypeStruct((rows, CW), BF16)],
        compiler_params=_cparams(("arbitrary",)),
        name="proj_conv",
    )(hm, w)


def _p3_kernel(hm_ref, w_ref, b_ref, g_ref):
    for rs in _subtiles(hm_ref):
        z = _dot(hm_ref[rs, :], w_ref[...]) + b_ref[...]
        g_ref[rs, :] = jax.nn.sigmoid(z).astype(BF16)


def _p3_call(hm, w, b_gate, rows):
    tm = 2 * PROJ_TM
    return pl.pallas_call(
        _p3_kernel,
        grid=(rows // tm,),
        in_specs=[pl.BlockSpec((tm, D), lambda i: (i, 0)), _const((D, 3 * D)), _const((1, 3 * D))],
        out_specs=pl.BlockSpec((tm, 3 * D), lambda i: (i, 0)),
        out_shape=jax.ShapeDtypeStruct((rows, 3 * D), BF16),
        compiler_params=_cparams(("arbitrary",)),
        name="proj_gate",
    )(hm, w, b_gate)


def _rope_fold(t, cs):
    rr = t * cs
    return rr + pltpu.roll(rr, MROPE, 1)


Q_SCALE = (MNOPE + MROPE) ** -0.5 * 1.4426950408889634
MLA_SUB = 512


def _p4_kernel(hm_ref, w_ref, cs_ref, qn_ref, wuq_ref, kvn_ref, wuk_ref, wuvt_ref, q_ref, k_ref, vt_ref):
    for rs in _subtiles(hm_ref, MLA_SUB):
        z = _dot(hm_ref[rs, :], w_ref[...])
        cs = cs_ref[rs, :]
        cqn = _rms(z[:, :MQR], qn_ref[...]).astype(BF16)
        kvn = _rms(z[:, MQR:MQR + MKVR], kvn_ref[...]).astype(BF16)
        qz = _dot(cqn, wuq_ref[...]) * Q_SCALE
        kz = _dot(kvn, wuk_ref[...])
        vt = _dot_nt(wuvt_ref[...], kvn)
        for h in range(MH):
            base = h * MHD
            rot = _rope_fold(qz[:, base + MNOPE:base + MHD], cs)
            q_ref[h, rs, :] = jnp.concatenate([qz[:, base:base + MNOPE], rot], axis=1).astype(BF16)
        krot = _rope_fold(z[:, MQR + MKVR:], cs)
        lane = lax.broadcasted_iota(jnp.int32, krot.shape, 1)
        krot = jnp.where(lane < MROPE, krot, 0.0)
        for h in range(MH):
            k_ref[h, rs, :] = jnp.concatenate([kz[:, h * MNOPE:(h + 1) * MNOPE], krot], axis=1).astype(BF16)
            vt_ref[h, :MV, rs] = vt[h * MV:(h + 1) * MV, :].astype(BF16)
            vt_ref[h, MV:, rs] = jnp.ones((MVA - MV, MLA_SUB), BF16)


def _p4_call(hm, w, cs, qn, wuq, kvn, wuk, wuvt):
    tm = PROJ_TM
    return pl.pallas_call(
        _p4_kernel,
        grid=(NT // tm,),
        in_specs=[pl.BlockSpec((tm, D), lambda i: (i, 0)), _const((D, MQR + MKVR + 2 * MROPE)),
                  pl.BlockSpec((tm, 2 * MROPE), lambda i: (i, 0)),
                  _const((1, MQR)), _const((MQR, MH * MHD)),
                  _const((1, MKVR)), _const((MKVR, MH * MNOPE)), _const((MH * MV, MKVR))],
        out_specs=[pl.BlockSpec((MH, tm, MHD), lambda i: (0, i, 0)),
                   pl.BlockSpec((MH, tm, MHD), lambda i: (0, i, 0)),
                   pl.BlockSpec((MH, MVA, tm), lambda i: (0, 0, i))],
        out_shape=[jax.ShapeDtypeStruct((MH, NT, MHD), BF16),
                   jax.ShapeDtypeStruct((MH, NT, MHD), BF16),
                   jax.ShapeDtypeStruct((MH, MVA, NT), BF16)],
        compiler_params=_cparams(("arbitrary",)),
        name="proj_mla",
    )(hm, w, cs, qn, wuq, kvn, wuk, wuvt)


GLA_TILE = 256
GLA_NCH = GLA_TILE // GCHUNK
GLA_NB = 2
assert LC == GLA_TILE


def _gla_local(qk_ref, v_ref, g_ref, backward):
    g = g_ref[...]
    ri = lax.broadcasted_iota(jnp.int32, (GLA_TILE, GLA_TILE), 0)
    ci = lax.broadcasted_iota(jnp.int32, (GLA_TILE, GLA_TILE), 1)
    same = (ri // GCHUNK) == (ci // GCHUNK)
    tri = jnp.where(same & ((ci >= ri) if backward else (ci <= ri)), 1.0, 0.0).astype(BF16)
    g_hi = g.astype(BF16)
    g_lo = (g - g_hi.astype(F32)).astype(BF16)
    bc = _dot(tri, g_hi) + _dot(tri, g_lo)
    r64 = lax.broadcasted_iota(jnp.int32, (GCHUNK, GCHUNK), 0)
    c64 = lax.broadcasted_iota(jnp.int32, (GCHUNK, GCHUNK), 1)
    mask = (c64 >= r64) if backward else (c64 <= r64)
    order = range(GLA_NCH - 1, -1, -1) if backward else range(GLA_NCH)
    items = [dict(r0=c * GCHUNK, h=h) for c in order for h in range(GH)]

    def decay_and_intra(it):
        r0, h = it["r0"], it["h"]
        b = bc[r0:r0 + GCHUNK, h * GDK:(h + 1) * GDK]
        bl = b[0:1] if backward else b[GCHUNK - 1:GCHUNK]
        q = qk_ref[r0:r0 + GCHUNK, h * GDK:(h + 1) * GDK].astype(F32)
        k = qk_ref[r0:r0 + GCHUNK, GKEY + h * GDK:GKEY + (h + 1) * GDK].astype(F32)
        qe = (q * jnp.exp(b)).astype(BF16)
        ke = (k * jnp.exp(-b)).astype(BF16)
        it.update(qe=qe, kd=(k * jnp.exp(bl - b)).astype(BF16), dec=jnp.exp(bl),
                  a=jnp.where(mask, _dot_nt(qe, ke), 0.0).astype(BF16))

    def value_products(it):
        v = v_ref[it["r0"]:it["r0"] + GCHUNK, it["h"] * GDV:(it["h"] + 1) * GDV]
        it["oi"] = _dot(it.pop("a"), v)
        it["u"] = _dot_tn(v, it.pop("kd"))

    steps = [functools.partial(decay_and_intra, it) for it in items]
    steps += [functools.partial(value_products, it) for it in items]
    return items, steps


def _gla_advance(scans, state):
    def sequential(i, items, store):
        def step(it):
            h = it["h"]
            s = state[i][h]
            store(it["r0"], h, it["oi"] + _dot_nt(it["qe"], s.astype(BF16)))
            state[i][h] = s * it["dec"] + it["u"]
        return [functools.partial(step, it) for it in items]

    pending = []
    for i, (qk_ref, v_ref, g_ref, backward, store) in enumerate(scans):
        items, local = _gla_local(qk_ref, v_ref, g_ref, backward)
        per = -(-len(local) // max(len(pending), 1))
        while pending or local:
            if pending:
                pending.pop(0)()
            for _ in range(min(per, len(local))):
                local.pop(0)()
        pending = sequential(i, items, store)
    for step in pending:
        step()
    return state


def _gla_store(o_ref, lead=()):
    def store(r0, h, val):
        o_ref[(*lead, slice(r0, r0 + GCHUNK), slice(h * GDV, (h + 1) * GDV))] = val
    return store


def _gla_ctx_kernel(qk_ref, v_ref, gf_ref, gb_ref, of_ref, ob_ref, s_ref):
    scans = [(qk_ref, v_ref, gf_ref, False, _gla_store(of_ref)),
             (qk_ref, v_ref, gb_ref, True, _gla_store(ob_ref))]
    zero = jnp.zeros((GDV, GDK), F32)
    state = _gla_advance(scans, [[zero] * GH for _ in scans])
    for d in range(2):
        for h in range(GH):
            s_ref[d, h] = state[d][h]


def _gla_x_kernel(*refs):
    ins, (s0_ref, of_ref, ob_ref, s_ref) = refs[:6 * GLA_NB], refs[6 * GLA_NB:]

    @pl.when(pl.program_id(1) == 0)
    def _():
        s_ref[...] = s0_ref[...]

    scans, slots = [], []
    for nb in range(GLA_NB):
        qkf, vf, gf, qkb, vb, gb = ins[6 * nb:6 * nb + 6]
        scans.append((qkf, vf, gf, False, _gla_store(of_ref, (nb,))))
        scans.append((qkb, vb, gb, True, _gla_store(ob_ref, (nb,))))
        slots += [(nb, 0), (nb, 1)]
    state = _gla_advance(scans, [[s_ref[nb, d, h] for h in range(GH)] for nb, d in slots])
    for (nb, d), st in zip(slots, state):
        for h in range(GH):
            s_ref[nb, d, h] = st[h]


def _gla_call(qk, v, lfb):
    ctx_blk = NX // LC
    of_c, ob_c, s_ctx = pl.pallas_call(
        _gla_ctx_kernel,
        grid=(B,),
        in_specs=[pl.BlockSpec((LC, 2 * GKEY), lambda b: (ctx_blk + b, 0)),
                  pl.BlockSpec((LC, GVAL), lambda b: (ctx_blk + b, 0)),
                  pl.BlockSpec((LC, GKEY), lambda b: (ctx_blk + b, 0)),
                  pl.BlockSpec((LC, GKEY), lambda b: (ctx_blk + b, 1))],
        out_specs=[pl.BlockSpec((LC, GVAL), lambda b: (b, 0)),
                   pl.BlockSpec((LC, GVAL), lambda b: (b, 0)),
                   pl.BlockSpec((None, 2, GH, GDV, GDK), lambda b: (b, 0, 0, 0, 0))],
        out_shape=[jax.ShapeDtypeStruct((NC, GVAL), F32), jax.ShapeDtypeStruct((NC, GVAL), F32),
                   jax.ShapeDtypeStruct((B, 2, GH, GDV, GDK), F32)],
        compiler_params=_cparams(("arbitrary",)),
        name="gla_ctx",
    )(qk, v, lfb, lfb)

    nxt = L // GLA_TILE
    in_specs, args = [], []
    for nb in range(GLA_NB):
        for d, tile in ((0, lambda t: t), (1, lambda t: nxt - 1 - t)):
            blk = lambda k, t, nb=nb, tile=tile: (k * GLA_NB + nb) * nxt + tile(t)
            in_specs += [pl.BlockSpec((GLA_TILE, 2 * GKEY), lambda k, t, blk=blk: (blk(k, t), 0)),
                         pl.BlockSpec((GLA_TILE, GVAL), lambda k, t, blk=blk: (blk(k, t), 0)),
                         pl.BlockSpec((GLA_TILE, GKEY), lambda k, t, blk=blk, d=d: (blk(k, t), d))]
            args += [qk, v, lfb]
    in_specs.append(pl.BlockSpec((GLA_NB, 2, GH, GDV, GDK), lambda k, t: (k, 0, 0, 0, 0)))
    of_x, ob_x = pl.pallas_call(
        _gla_x_kernel,
        grid=(B // GLA_NB, nxt),
        in_specs=in_specs,
        out_specs=[pl.BlockSpec((GLA_NB, GLA_TILE, GVAL), lambda k, t: (k, t, 0)),
                   pl.BlockSpec((GLA_NB, GLA_TILE, GVAL), lambda k, t: (k, nxt - 1 - t, 0))],
        out_shape=[jax.ShapeDtypeStruct((B, L, GVAL), F32), jax.ShapeDtypeStruct((B, L, GVAL), F32)],
        scratch_shapes=[pltpu.VMEM((GLA_NB, 2, GH, GDV, GDK), F32)],
        compiler_params=_cparams(("arbitrary", "arbitrary")),
        name="gla",
    )(*args, s_ctx)
    return of_x.reshape(NX, GVAL), ob_x.reshape(NX, GVAL), of_c, ob_c


ATT_TQ = 1024
ATT_KC = (2048, 2048)
ATT_KC_EXACT = (512,) * 8
assert sum(ATT_KC) == L and sum(ATT_KC_EXACT) == L
ATT_AHEAD = 2
ATT_SLACK = 64.0
ATT_HP = 2


def _flash_t(streams, lazy_max, ahead_n):
    n = len(streams[0][1])
    m = [None] * len(streams)
    acc = [None] * len(streams)
    excess = [None] * len(streams)
    ahead = [[_dot_nt(ch[j][0], q) for j in range(min(ahead_n, n))] for q, ch in streams]
    for j in range(n):
        for i, (q, ch) in enumerate(streams):
            s = ahead[i].pop(0)
            if j + ahead_n < n:
                ahead[i].append(_dot_nt(ch[j + ahead_n][0], q))
            vt = ch[j][1]
            cm = jnp.max(s, axis=0, keepdims=True)
            if j == 0:
                m[i] = cm
                acc[i] = _dot(vt, jnp.exp2(s - cm).astype(BF16))
                continue
            m_new = jnp.maximum(m[i], cm)
            if lazy_max:
                over = cm - m[i]
                excess[i] = over if excess[i] is None else jnp.maximum(excess[i], over)
                acc[i] = (acc[i] + _dot(vt, jnp.exp2(s - m[i]).astype(BF16))) * jnp.exp2(m[i] - m_new)
            else:
                acc[i] = jnp.exp2(m[i] - m_new) * acc[i] + _dot(vt, jnp.exp2(s - m_new).astype(BF16))
            m[i] = m_new
    return [a[:MV, :] / a[MV:MV + 1, :] for a in acc], excess


def _attn_kernel(q_ref, kx_ref, kc_ref, vtx_ref, vtc_ref, o_ref):
    def run(lazy_max, ahead_n, sizes):
        streams = []
        for h in range(ATT_HP):
            chunks, k0 = [(kc_ref[h], vtc_ref[h])], 0
            for n in sizes:
                chunks.append((kx_ref[h, k0:k0 + n, :], vtx_ref[h, :, k0:k0 + n]))
                k0 += n
            streams.append((q_ref[h], chunks))
        outs, excess = _flash_t(streams, lazy_max, ahead_n)
        for h, o in enumerate(outs):
            o_ref[:, h * MV:(h + 1) * MV] = o.T.astype(BF16)
        return excess

    excess = run(True, 1, ATT_KC)
    worst = functools.reduce(jnp.maximum, [jnp.max(e) for e in excess])

    @pl.when(worst > ATT_SLACK)
    def _():
        run(False, ATT_AHEAD, ATT_KC_EXACT)


def _attn_ctx_kernel(q_ref, kc_ref, vtc_ref, o_ref):
    outs, _ = _flash_t([(q_ref[h], [(kc_ref[h], vtc_ref[h])]) for h in range(MH)], False, 1)
    for h, o in enumerate(outs):
        o_ref[:, h * MV:(h + 1) * MV] = o.T.astype(BF16)


def _attn_call(qa, ka, vta, with_ctx):
    nq = L // ATT_TQ
    hp = ATT_HP
    out = pl.pallas_call(
        _attn_kernel,
        grid=(B, MH // hp, nq),
        in_specs=[pl.BlockSpec((hp, ATT_TQ, MHD), lambda b, h, i: (h, b * nq + i, 0)),
                  pl.BlockSpec((hp, L, MHD), lambda b, h, i: (h, b, 0)),
                  pl.BlockSpec((hp, LC, MHD), lambda b, h, i: (h, NX // LC + b, 0)),
                  pl.BlockSpec((hp, MVA, L), lambda b, h, i: (h, 0, b)),
                  pl.BlockSpec((hp, MVA, LC), lambda b, h, i: (h, 0, NX // LC + b))],
        out_specs=pl.BlockSpec((ATT_TQ, hp * MV), lambda b, h, i: (b * nq + i, h)),
        out_shape=jax.ShapeDtypeStruct((NX, MH * MV), BF16),
        compiler_params=_cparams(("arbitrary", "arbitrary", "arbitrary")),
        name="attn",
    )(qa, ka, ka, vta, vta)
    if not with_ctx:
        return out, out
    out_c = pl.pallas_call(
        _attn_ctx_kernel,
        grid=(B,),
        in_specs=[pl.BlockSpec((MH, LC, MHD), lambda b: (0, NX // LC + b, 0)),
                  pl.BlockSpec((MH, LC, MHD), lambda b: (0, NX // LC + b, 0)),
                  pl.BlockSpec((MH, MVA, LC), lambda b: (0, 0, NX // LC + b))],
        out_specs=pl.BlockSpec((LC, MH * MV), lambda b: (b, 0)),
        out_shape=jax.ShapeDtypeStruct((NC, MH * MV), BF16),
        compiler_params=_cparams(("arbitrary",)),
        name="attn_ctx",
    )(qa, ka, vta)
    return out, out_c


MERGE_TM = 512


def _merge_kernel(x_ref, ofx_ref, obx_ref, ofc_ref, obc_ref, r_ref, gn_ref, zb_ref, u_ref, up_ref, un_ref,
                  cw_ref, cx_ref, cc_ref, gt_ref, wa_ref, wb_ref, wc_ref, wo_ref, g5_ref, lw_ref, lb_ref, o_ref):
    tm = MERGE_TM
    c = _dot(jnp.where(pl.program_id(0) < NX // tm, cx_ref[...], cc_ref[...]), wc_ref[...])
    o = jnp.where(pl.program_id(0) < NX // tm, ofx_ref[...] + obx_ref[...], ofc_ref[...] + obc_ref[...])
    parts = []
    for h in range(GH):
        oh = o[:, h * GDV:(h + 1) * GDV]
        parts.append(oh * lax.rsqrt(jnp.mean(oh * oh, axis=-1, keepdims=True) + EPS))
    a_in = (jnp.concatenate(parts, axis=1) * gn_ref[...] * r_ref[...].astype(F32)).astype(BF16)
    u = u_ref[...].astype(F32)
    prev = up_ref[HALO_ROWS - 1:HALO_ROWS, :].astype(F32)
    nxt = un_ref[0:1, :].astype(F32)
    loc = lax.broadcasted_iota(jnp.int32, (tm, 1), 0)
    row = loc + pl.program_id(0) * tm
    seq = jnp.where(row >= NX, LC, L)
    pos = row & (seq - 1)
    ul = jnp.where(loc == 0, prev, pltpu.roll(u, 1, 0))
    ul = jnp.where(pos == 0, 0.0, ul)
    ur = jnp.where(loc == tm - 1, nxt, pltpu.roll(u, tm - 1, 0))
    ur = jnp.where(pos == seq - 1, 0.0, ur)
    cv = ul * cw_ref[0:1, :] + u * cw_ref[1:2, :] + ur * cw_ref[2:3, :]
    b_in = (zb_ref[...].astype(F32) * cv).astype(BF16)
    subs = _subtiles(x_ref)
    ab = [(_dot(a_in[rs, :], wa_ref[...]), _dot(b_in[rs, :], wb_ref[...])) for rs in subs]
    for rs, (a, bb) in zip(subs, ab):
        mix = (gt_ref[rs, :D].astype(F32) * a + gt_ref[rs, D:2 * D].astype(F32) * bb
               + gt_ref[rs, 2 * D:].astype(F32) * c[rs, :])
        mx = _dot(mix.astype(BF16), wo_ref[...])
        z = ALPHA * x_ref[rs, :] + g5_ref[...] * mx
        o_ref[rs, :] = _layer_norm(z, lw_ref[...], lb_ref[...])


def _merge_call(x1, o_gla, r, gnorm, zb, u, conv_w, c_pre, gates, wa, wb, wc, wo, g5, lw, lb, rows):
    tm = MERGE_TM
    row = lambda w: pl.BlockSpec((tm, w), lambda i: (i, 0))
    nxt = NX // tm
    lat = pl.BlockSpec((tm, GVAL), lambda i: (jnp.minimum(i, nxt - 1), 0))
    ctx = pl.BlockSpec((tm, GVAL), lambda i: (jnp.maximum(i - nxt, 0), 0))
    per, last_halo = tm // HALO_ROWS, rows // HALO_ROWS - 1
    return pl.pallas_call(
        _merge_kernel,
        grid=(rows // tm,),
        in_specs=[row(D), lat, lat, ctx, ctx, row(GVAL), _const((1, GVAL)),
                  row(CW), row(CW),
                  pl.BlockSpec((HALO_ROWS, CW), lambda i: (jnp.maximum(i * per - 1, 0), 0)),
                  pl.BlockSpec((HALO_ROWS, CW), lambda i: (jnp.minimum((i + 1) * per, last_halo), 0)),
                  _const((3, CW)),
                  lat, ctx, row(3 * D),
                  _const((GVAL, D)), _const((CW, D)), _const((MH * MV, D)), _const((D, D)),
                  _mod_spec(tm), _const((1, D)), _const((1, D))],
        out_specs=row(D),
        out_shape=jax.ShapeDtypeStruct((rows, D), F32),
        compiler_params=_cparams(("arbitrary",)),
        name="merge",
    )(x1, *o_gla, r, gnorm, zb, u, u, u, conv_w, *c_pre, gates, wa, wb, wc, wo, g5, lw, lb)


def _rope_tables():
    t = jnp.arange(L, dtype=jnp.int32)
    quarter = MROPE // 4
    inv_freq = THETA ** (-jnp.arange(quarter, dtype=F32) / quarter)

    def cs(pos):
        ang = pos.astype(F32)[:, None] * inv_freq
        return jnp.cos(ang), jnp.sin(ang)

    cr, sr = cs(t // GRID_W)
    cc, sn = cs(t % GRID_W)
    tab = jnp.concatenate([cr, cr, cc, cc, -sr, sr, -sn, sn], axis=1)
    ident = jnp.concatenate([jnp.ones((NC, MROPE), F32), jnp.zeros((NC, MROPE), F32)], axis=1)
    return jnp.concatenate([jnp.tile(tab, (B, 1)), ident], axis=0)


def _swap_rope_cols(w):
    q = MROPE // 4
    return jnp.concatenate([w[..., q:2 * q], w[..., :q], w[..., 3 * q:], w[..., 2 * q:3 * q]], axis=-1)


def _layer_weights(l, w_in, gla_decay_w, gla_decay_b, mla_w_uq, mla_w_ukv):
    wi = w_in[l]
    o = 0
    cols = {}
    for name, n in (("q", GKEY), ("k", GKEY), ("v", GVAL), ("r", GVAL), ("gf", GRANK), ("gb", GRANK),
                    ("cb", CW), ("cc", CW), ("cx", CW), ("cq", MQR), ("ckv", MKVR), ("kr", MROPE),
                    ("zg", 3 * D)):
        cols[name] = wi[:, o:o + n]
        o += n
    pad = jnp.zeros((D, GRANK_PAD - 2 * GRANK), F32)
    w1 = jnp.concatenate([cols["gf"], cols["gb"], pad, cols["q"], cols["k"], cols["v"], cols["r"]], axis=1)
    w2 = jnp.concatenate([cols["cb"], cols["cc"], cols["cx"]], axis=1)
    w4 = jnp.concatenate([cols["cq"], cols["ckv"], cols["kr"], _swap_rope_cols(cols["kr"])], axis=1)
    dw = jnp.zeros((GRANK_PAD, 2 * GKEY), F32)
    dw = dw.at[:GRANK, :GKEY].set(gla_decay_w[l, 0]).at[GRANK:2 * GRANK, GKEY:].set(gla_decay_w[l, 1])
    db = jnp.concatenate([gla_decay_b[l, 0], gla_decay_b[l, 1]])[None, :]
    wq = mla_w_uq[l].reshape(MQR, MH, MNOPE + MROPE)
    wq = jnp.concatenate([wq, _swap_rope_cols(wq[..., MNOPE:])], axis=-1).reshape(MQR, MH * MHD)
    wkv = mla_w_ukv[l].reshape(MKVR, MH, MNOPE + MV)
    wk = wkv[..., :MNOPE].reshape(MKVR, MH * MNOPE)
    wvt = wkv[..., MNOPE:].reshape(MKVR, MH * MV).T
    bf = lambda a: a.astype(BF16)
    return dict(w1=bf(w1), w2=bf(w2), w3=bf(cols["zg"]), w4=bf(w4), dw=bf(dw), db=db, wq=bf(wq),
                wk=bf(wk), wvt=bf(wvt))


def kernel(x, c, ctx, c_ctx, ada_w, ada_b, ln_w, ln_b, ffn_w13, ffn_w2, w_in, b_gate, gla_decay_w,
           gla_decay_b, gla_norm, gla_proj, conv_w, conv_proj, mla_q_norm, mla_w_uq, mla_kv_norm,
           mla_w_ukv, mla_proj, w_out):
    assert x.shape == (B, L, D) and ctx.shape == (B, LC, D)
    c_all = jnp.concatenate([c, c_ctx[None, :], jnp.zeros((MODROWS - B - 1, D), F32)], axis=0)
    mods = _ada_call(c_all, ada_w, ada_b).reshape(DEPTH, MODROWS, NMOD, 1, D)
    cs = _rope_tables()
    h = (x.reshape(NX, D), ctx.reshape(NC, D))
    vec = lambda a: a.reshape(1, -1)
    for l in range(DEPTH):
        last = l == DEPTH - 1
        rows = NX if last else NT
        m = [mods[l, :, k] for k in range(NMOD)]
        lw = _layer_weights(l, w_in, gla_decay_w, gla_decay_b, mla_w_uq, mla_w_ukv)
        x1, hm = _ffn_call(h, NT, m[0:3], ffn_w13[l, 0].astype(BF16), ffn_w2[l, 0].astype(BF16),
                           vec(ln_w[l, 0]), vec(ln_b[l, 0]), mod_next=(m[3], m[4]))
        qk, v, r, lfb = _p1_call(hm, lw["w1"], lw["dw"], lw["db"])
        zb, u = _p2_call(hm, lw["w2"], rows)
        gates = _p3_call(hm, lw["w3"], vec(b_gate[l]), rows)
        qa, ka, vta = _p4_call(hm, lw["w4"], cs, vec(mla_q_norm[l]), lw["wq"], vec(mla_kv_norm[l]),
                               lw["wk"], lw["wvt"])
        o_gla = _gla_call(qk, v, lfb)
        c_pre = _attn_call(qa, ka, vta, with_ctx=not last)
        x2 = _merge_call(x1, o_gla, r, vec(gla_norm[l]), zb, u, conv_w[l], c_pre, gates,
                         gla_proj[l].astype(BF16), conv_proj[l].astype(BF16), mla_proj[l].astype(BF16),
                         w_out[l].astype(BF16), m[5], vec(ln_w[l, 1]), vec(ln_b[l, 1]), rows)
        h = _ffn_call(x2, rows, m[6:9], ffn_w13[l, 1].astype(BF16), ffn_w2[l, 1].astype(BF16),
                      vec(ln_w[l, 2]), vec(ln_b[l, 2]))
    return h.reshape(B, L, D)
```
